```python
import jax, jax.numpy as jnp
from jax import lax
import numpy as np

D_MODEL = 1024
BATCH = 16
SEQ = 2048
DEPTH = 2

MLA_HEADS = 6
MLA_NOPE = 64
MLA_ROPE = 32
MLA_V = 64
MLA_Q_RANK = 256
MLA_KV_RANK = 128
ATTN_Q_BLOCK = 128
RET_HEADS = 5
RET_DK = 64
RET_DV = 64
RET_CHUNK = 128
MOBA_HEADS = 5
MOBA_DH = 64
MOBA_BLOCK = 256
MOBA_TOPK = 3
MOBA_Q_CHUNK = 64
ROPE_THETA = 10000.0
LN_EPS = 1e-5
RMS_EPS = 1e-6
D_FF = 2816
N_EXPERTS = 8
TOP_K = 2
D_FF_EXPERT = 3584
DN_ALPHA = (2 * DEPTH) ** 0.25
DN_BETA = (8 * DEPTH) ** -0.25

W_MLA_OUT = MLA_HEADS * MLA_V
W_RET_OUT = RET_HEADS * RET_DV
W_MOBA_OUT = MOBA_HEADS * MOBA_DH
MIX_WIDTH = W_MLA_OUT + W_RET_OUT + W_MOBA_OUT
IN_SIZES = (MLA_Q_RANK, MLA_KV_RANK, MLA_ROPE,
            RET_HEADS * RET_DK, RET_HEADS * RET_DK, RET_HEADS * RET_DV, RET_HEADS * RET_DV,
            MOBA_HEADS * MOBA_DH, MOBA_HEADS * MOBA_DH, MOBA_HEADS * MOBA_DH)
D_IN = sum(IN_SIZES)
N_DENSE = (DEPTH + 1) // 2
N_MOE = DEPTH // 2

kernel_name = "hymba_mla_retnet_moba_deepnorm_moe"


def layer_norm(x, g, b):
    xf = x.astype(jnp.float32)
    mu = jnp.mean(xf, -1, keepdims=True)
    var = jnp.mean(jnp.square(xf - mu), -1, keepdims=True)
    return ((xf - mu) * lax.rsqrt(var + LN_EPS) * g.astype(jnp.float32) + b.astype(jnp.float32)).astype(x.dtype)


def rms_norm(x, g):
    xf = x.astype(jnp.float32)
    y = xf * lax.rsqrt(jnp.mean(jnp.square(xf), -1, keepdims=True) + RMS_EPS)
    return (y * g.astype(jnp.float32)).astype(x.dtype)


def rope_tables(n_pos, dim):
    pos = jnp.arange(n_pos, dtype=jnp.float32)
    inv = ROPE_THETA ** (-jnp.arange(0, dim, 2, dtype=jnp.float32) / dim)
    ang = pos[:, None] * inv[None, :]
    return jnp.cos(ang), jnp.sin(ang)


def apply_rope(x, cos, sin):
    shp = (1, cos.shape[0]) + (1,) * (x.ndim - 3) + (cos.shape[1],)
    c = cos.reshape(shp).astype(x.dtype)
    s = sin.reshape(shp).astype(x.dtype)
    x1, x2 = jnp.split(x, 2, axis=-1)
    return jnp.concatenate([x1 * c - x2 * s, x1 * s + x2 * c], axis=-1)


def mla_attention(c_q, c_kv, k_rope, q_norm_g, kv_norm_g, w_uq, w_ukv, cos, sin):
    B, S, _ = c_q.shape
    H, QB = MLA_HEADS, ATTN_Q_BLOCK
    q = (rms_norm(c_q, q_norm_g) @ w_uq).reshape(B, S, H, MLA_NOPE + MLA_ROPE)
    q_nope = q[..., :MLA_NOPE]
    q_pe = apply_rope(q[..., MLA_NOPE:], cos, sin)
    kv = (rms_norm(c_kv, kv_norm_g) @ w_ukv).reshape(B, S, H, MLA_NOPE + MLA_V)
    k_nope, v = kv[..., :MLA_NOPE], kv[..., MLA_NOPE:]
    k_pe = apply_rope(k_rope, cos, sin)
    scale = (MLA_NOPE + MLA_ROPE) ** -0.5
    nq = S // QB
    qn_blocks = q_nope.reshape(B, nq, QB, H, MLA_NOPE).transpose(1, 0, 2, 3, 4)
    qp_blocks = q_pe.reshape(B, nq, QB, H, MLA_ROPE).transpose(1, 0, 2, 3, 4)
    k_pos = jnp.arange(S)

    def block(args):
        qn, qp, start = args
        s = (jnp.einsum('bqhd,bkhd->bhqk', qn, k_nope)
             + jnp.einsum('bqhd,bkd->bhqk', qp, k_pe)).astype(jnp.float32) * scale
        q_pos = start + jnp.arange(QB)
        s = jnp.where(k_pos[None, :] <= q_pos[:, None], s, -jnp.inf)
        p = jax.nn.softmax(s, axis=-1).astype(v.dtype)
        return jnp.einsum('bhqk,bkhd->bqhd', p, v)

    o = lax.map(block, (qn_blocks, qp_blocks, jnp.arange(nq) * QB))
    return o.transpose(1, 0, 2, 3, 4).reshape(B, S, W_MLA_OUT)


def retention(q, k, v, g, cos, sin):
    B, S, _ = q.shape
    H, C = RET_HEADS, RET_CHUNK
    nc = S // C
    f32 = jnp.float32
    q = apply_rope(q.reshape(B, S, H, RET_DK), cos, sin)
    k = apply_rope(k.reshape(B, S, H, RET_DK), cos, sin) * (RET_DK ** -0.5)
    v = v.reshape(B, S, H, RET_DV)
    log_gamma = jnp.log(1.0 - 2.0 ** (-5.0 - jnp.arange(H, dtype=f32)))
    qc = q.reshape(B, nc, C, H, RET_DK)
    kc = k.reshape(B, nc, C, H, RET_DK)
    vc = v.reshape(B, nc, C, H, RET_DV)
    idx = jnp.arange(C, dtype=f32)
    diff = idx[:, None] - idx[None, :]
    decay_in = jnp.where(diff >= 0, jnp.exp(log_gamma[:, None, None] * jnp.maximum(diff, 0.0)), 0.0)
    intra = jnp.einsum('bnihd,bnjhd->bnhij', qc, kc).astype(f32) * decay_in
    o_intra = jnp.einsum('bnhij,bnjhe->bnihe', intra.astype(v.dtype), vc).astype(f32)
    k_decay = jnp.exp(log_gamma[:, None] * (C - 1.0 - idx))
    kv = jnp.einsum('bnjhd,hj,bnjhe->nbhde', kc.astype(f32), k_decay, vc.astype(f32))
    chunk_decay = jnp.exp(log_gamma * C)[None, :, None, None]

    def step(R, kv_n):
        return R * chunk_decay + kv_n, R

    _, R_prev = lax.scan(step, jnp.zeros((B, H, RET_DK, RET_DV), f32), kv)
    q_decay = jnp.exp(log_gamma[:, None] * (idx + 1.0))
    o_cross = jnp.einsum('bnihd,nbhde,hi->bnihe', qc.astype(f32), R_prev, q_decay)
    o = (o_intra + o_cross).reshape(B, S, H, RET_DV)
    mu = jnp.mean(o, -1, keepdims=True)
    var = jnp.mean(jnp.square(o - mu), -1, keepdims=True)
    o = ((o - mu) * lax.rsqrt(var + RMS_EPS)).reshape(B, S, W_RET_OUT)
    return (jax.nn.silu(g.astype(f32)) * o).astype(q.dtype)


def moba_attention(q, k, v):
    B, S, _ = q.shape
    H, dh, BLK, QC = MOBA_HEADS, MOBA_DH, MOBA_BLOCK, MOBA_Q_CHUNK
    f32 = jnp.float32
    nb = -(-S // BLK)
    Sp = nb * BLK
    pad = Sp - S

    def heads(t):
        t = jnp.pad(t.reshape(B, S, H, dh), ((0, 0), (0, pad), (0, 0), (0, 0)))
        return t.transpose(0, 2, 1, 3)

    q, k, v = heads(q), heads(k), heads(v)
    k_blk = k.reshape(B, H, nb, BLK, dh)
    v_blk = v.reshape(B, H, nb, BLK, dh)
    k_mean = jnp.mean(k_blk.astype(f32), axis=3)
    slopes = 2.0 ** (-8.0 * (jnp.arange(H, dtype=f32) + 1.0) / H)
    n_sel = min(MOBA_TOPK, nb - 1)
    scale = dh ** -0.5
    nqc = Sp // QC
    q_chunks = q.reshape(B, H, nqc, QC, dh).transpose(2, 0, 1, 3, 4)
    bi = jnp.arange(B)[:, None, None, None]
    hi = jnp.arange(H)[None, :, None, None]
    blk_off = jnp.arange(BLK)

    def chunk(args):
        qc, start = args
        blk = start // BLK
        q_pos = start + jnp.arange(QC)
        k_own = lax.dynamic_index_in_dim(k_blk, blk, axis=2, keepdims=False)
        v_own = lax.dynamic_index_in_dim(v_blk, blk, axis=2, keepdims=False)
        own_pos = blk * BLK + blk_off
        s_own = (jnp.einsum('bhqd,bhkd->bhqk', qc, k_own).astype(f32) * scale
                 - slopes[:, None, None] * (q_pos[:, None] - own_pos[None, :]).astype(f32))
        s_own = jnp.where(own_pos[None, :] <= q_pos[:, None], s_own, -jnp.inf)
        if n_sel == 0:
            p = jax.nn.softmax(s_own, axis=-1).astype(v.dtype)
            return jnp.einsum('bhqk,bhkd->bhqd', p, v_own)
        gate = jnp.einsum('bhqd,bhnd->bhqn', qc.astype(f32), k_mean)
        gate = jnp.where(jnp.arange(nb) < blk, gate, -jnp.inf)
        _, sel = lax.top_k(gate, n_sel)
        k_sel = k_blk[bi, hi, sel]
        v_sel = v_blk[bi, hi, sel]
        sel_pos = sel[..., None] * BLK + blk_off
        s_sel = (jnp.einsum('bhqd,bhqnkd->bhqnk', qc, k_sel).astype(f32) * scale
                 - slopes[:, None, None, None] * (q_pos[:, None, None] - sel_pos).astype(f32))
        slot_ok = jnp.arange(n_sel) < blk
        s_sel = jnp.where(slot_ok[:, None], s_sel, -jnp.inf)
        s_all = jnp.concatenate([s_sel.reshape(B, H, QC, n_sel * BLK), s_own], axis=-1)
        p = jax.nn.softmax(s_all, axis=-1).astype(v.dtype)
        p_sel = p[..., :n_sel * BLK].reshape(B, H, QC, n_sel, BLK)
        return (jnp.einsum('bhqnk,bhqnkd->bhqd', p_sel, v_sel)
                + jnp.einsum('bhqk,bhkd->bhqd', p[..., n_sel * BLK:], v_own))

    o = lax.map(chunk, (q_chunks, jnp.arange(nqc) * QC))
    o = o.transpose(1, 0, 3, 2, 4).reshape(B, Sp, W_MOBA_OUT)
    return o[:, :S]


def hybrid_mixer(x, w_in, q_norm_g, kv_norm_g, w_uq, w_ukv, beta_mla, beta_moba, w_o,
                 cos_a, sin_a, cos_b, sin_b):
    h = x @ w_in
    splits = [int(s) for s in np.cumsum(IN_SIZES)[:-1]]
    c_q, c_kv, k_rope, rq, rk, rv, rg, mq, mk, mv = jnp.split(h, splits, axis=-1)
    o_a = rms_norm(mla_attention(c_q, c_kv, k_rope, q_norm_g, kv_norm_g, w_uq, w_ukv, cos_a, sin_a), beta_mla)
    o_b = retention(rq, rk, rv, rg, cos_b, sin_b)
    o_c = rms_norm(moba_attention(mq, mk, mv), beta_moba)
    return jnp.concatenate([o_a, o_b, o_c], axis=-1) @ w_o


def swiglu(x, w_gate, w_up, w_down):
    return (jax.nn.silu(x @ w_gate) * (x @ w_up)) @ w_down


def moe_swiglu(x, w_router, w_gate, w_up, w_down):
    B, S, D = x.shape
    xt = x.reshape(B * S, D)
    logits = (xt @ w_router).astype(jnp.float32)
    top_v, top_i = lax.top_k(logits, TOP_K)
    top_w = jax.nn.softmax(top_v, axis=-1)
    gates = jnp.sum(jax.nn.one_hot(top_i, N_EXPERTS, dtype=jnp.float32) * top_w[..., None], axis=1)
    y = jnp.zeros((B * S, D), jnp.float32)
    for e in range(N_EXPERTS):
        y = y + gates[:, e:e + 1] * swiglu(xt, w_gate[e], w_up[e], w_down[e]).astype(jnp.float32)
    return y.astype(x.dtype).reshape(B, S, D)


def setup_inputs(seed: int = 0) -> dict:
    key = jax.random.key(seed)
    ks = jax.random.split(key, 24)
    f32 = jnp.float32
    nrm = lambda k, shp, s: jax.random.normal(k, shp, f32) * s
    x = jax.random.normal(ks[0], (BATCH, SEQ, D_MODEL), f32)
    col_scale = jnp.concatenate([
        jnp.full((s,), DN_BETA if i in (5, 9) else 1.0, f32) for i, s in enumerate(IN_SIZES)])
    w_in = nrm(ks[1], (DEPTH, D_MODEL, D_IN), D_MODEL ** -0.5) * col_scale
    ukv_scale = jnp.tile(jnp.concatenate([jnp.ones((MLA_NOPE,), f32), jnp.full((MLA_V,), DN_BETA, f32)]), MLA_HEADS)
    return {
        "x": x,
        "ln_emb_g": 1.0 + nrm(ks[2], (D_MODEL,), 0.02),
        "ln_emb_b": nrm(ks[3], (D_MODEL,), 0.02),
        "w_in": w_in,
        "q_norm_g": 1.0 + nrm(ks[4], (DEPTH, MLA_Q_RANK), 0.02),
        "kv_norm_g": 1.0 + nrm(ks[5], (DEPTH, MLA_KV_RANK), 0.02),
        "w_uq": nrm(ks[6], (DEPTH, MLA_Q_RANK, MLA_HEADS * (MLA_NOPE + MLA_ROPE)), MLA_Q_RANK ** -0.5),
        "w_ukv": nrm(ks[7], (DEPTH, MLA_KV_RANK, MLA_HEADS * (MLA_NOPE + MLA_V)), MLA_KV_RANK ** -0.5) * ukv_scale,
        "beta_mla": 1.0 + nrm(ks[8], (DEPTH, W_MLA_OUT), 0.02),
        "beta_moba": 1.0 + nrm(ks[9], (DEPTH, W_MOBA_OUT), 0.02),
        "w_o": nrm(ks[10], (DEPTH, MIX_WIDTH, D_MODEL), MIX_WIDTH ** -0.5 * DN_BETA),
        "ln1_g": 1.0 + nrm(ks[11], (DEPTH, D_MODEL), 0.02),
        "ln1_b": nrm(ks[12], (DEPTH, D_MODEL), 0.02),
        "ffn_w_gate": nrm(ks[13], (N_DENSE, D_MODEL, D_FF), D_MODEL ** -0.5 * DN_BETA),
        "ffn_w_up": nrm(ks[14], (N_DENSE, D_MODEL, D_FF), D_MODEL ** -0.5 * DN_BETA),
        "ffn_w_down": nrm(ks[15], (N_DENSE, D_FF, D_MODEL), D_FF ** -0.5 * DN_BETA),
        "router": nrm(ks[16], (N_MOE, D_MODEL, N_EXPERTS), D_MODEL ** -0.5),
        "exp_w_gate": nrm(ks[17], (N_MOE, N_EXPERTS, D_MODEL, D_FF_EXPERT), D_MODEL ** -0.5 * DN_BETA),
        "exp_w_up": nrm(ks[18], (N_MOE, N_EXPERTS, D_MODEL, D_FF_EXPERT), D_MODEL ** -0.5 * DN_BETA),
        "exp_w_down": nrm(ks[19], (N_MOE, N_EXPERTS, D_FF_EXPERT, D_MODEL), D_FF_EXPERT ** -0.5 * DN_BETA),
        "ln2_g": 1.0 + nrm(ks[20], (DEPTH, D_MODEL), 0.02),
        "ln2_b": nrm(ks[21], (DEPTH, D_MODEL), 0.02),
    }


def reference(x, ln_emb_g, ln_emb_b, w_in, q_norm_g, kv_norm_g, w_uq, w_ukv, beta_mla, beta_moba,
              w_o, ln1_g, ln1_b, ffn_w_gate, ffn_w_up, ffn_w_down, router, exp_w_gate, exp_w_up,
              exp_w_down, ln2_g, ln2_b):
    S = x.shape[1]
    cos_a, sin_a = rope_tables(S, MLA_ROPE)
    cos_b, sin_b = rope_tables(S, RET_DK)
    x = layer_norm(x, ln_emb_g, ln_emb_b)
    for l in range(DEPTH):
        mix = hybrid_mixer(x, w_in[l], q_norm_g[l], kv_norm_g[l], w_uq[l], w_ukv[l],
                           beta_mla[l], beta_moba[l], w_o[l], cos_a, sin_a, cos_b, sin_b)
        x = layer_norm(DN_ALPHA * x + mix, ln1_g[l], ln1_b[l])
        if l % 2 == 0:
            j = l // 2
            f = swiglu(x, ffn_w_gate[j], ffn_w_up[j], ffn_w_down[j])
        else:
            j = l // 2
            f = moe_swiglu(x, router[j], exp_w_gate[j], exp_w_up[j], exp_w_down[j])
        x = layer_norm(DN_ALPHA * x + f, ln2_g[l], ln2_b[l])
    return x
```

```python
import functools

import numpy as np
import jax
import jax.numpy as jnp
from jax import lax
from jax.experimental import pallas as pl
from jax.experimental.pallas import tpu as pltpu

F32 = jnp.float32
BF16 = jnp.bfloat16

MLA_HEADS, MLA_NOPE, MLA_ROPE, MLA_V = 6, 64, 32, 64
MLA_Q_RANK, MLA_KV_RANK = 256, 128
RET_HEADS, RET_DK, RET_DV, RET_CHUNK = 5, 64, 64, 128
MOBA_HEADS, MOBA_DH, MOBA_BLOCK, MOBA_TOPK = 5, 64, 256, 3
ROPE_THETA = 10000.0
LN_EPS = 1e-5
RMS_EPS = 1e-6
N_EXPERTS = 8
W_MLA_OUT = MLA_HEADS * MLA_V
W_RET = RET_HEADS * RET_DK
W_MOBA = MOBA_HEADS * MOBA_DH

LANES = 128
MLA_SLOT = 128
NEG = -1e30


def _pad_to(n, m):
    return -(-n // m) * m


RET_PAD = _pad_to(W_RET, LANES)
MOBA_PAD = _pad_to(W_MOBA, LANES)


def _layer_norm(x, g, b):
    mu = jnp.mean(x, -1, keepdims=True)
    xc = x - mu
    var = jnp.mean(xc * xc, -1, keepdims=True)
    return xc * lax.rsqrt(var + LN_EPS) * g + b


def _rms_norm(x, g):
    return x * lax.rsqrt(jnp.mean(x * x, -1, keepdims=True) + RMS_EPS) * g


def _silu(x):
    return x / (1.0 + jnp.exp(-x))


def _dot(a, b):
    return jnp.dot(a, b, preferred_element_type=F32)


def _dot_nt(a, b):
    return lax.dot_general(a, b, (((1,), (1,)), ((), ())), preferred_element_type=F32)


def _in_proj_kernel(*refs, apply_ln, blocks_per_tile):
    if apply_ln:
        x_ref, lng_ref, lnb_ref, *refs = refs
    else:
        x_ref, *refs = refs
    (w1_ref, w2_ref, w3_ref, qg_ref, kvg_ref, wuq_ref, wuqs_ref, wukv_ref, pk_ref,
     cq_ref, sq_ref, ck_ref, sk_ref, cr_ref, sr_ref, *outs) = refs
    if apply_ln:
        x0_ref, *outs = outs
    (qm_ref, km_ref, vm_ref, rq_ref, rk_ref, rv_ref, rg_ref, mq_ref, mk_ref, mv_ref, kmean_ref) = outs

    x = x_ref[...]
    if apply_ln:
        x = _layer_norm(x, lng_ref[...], lnb_ref[...])
        x0_ref[...] = x
    xb = x.astype(BF16)

    ha = _dot(xb, w1_ref[...])
    qn = _rms_norm(ha[:, :MLA_Q_RANK], qg_ref[...]).astype(BF16)
    q = _dot(qn, wuq_ref[...]) * cq_ref[...] + _dot(qn, wuqs_ref[...]) * sq_ref[...]
    qm_ref[...] = q.astype(BF16)
    kvn = _rms_norm(ha[:, MLA_Q_RANK:MLA_Q_RANK + MLA_KV_RANK], kvg_ref[...]).astype(BF16)
    kv = _dot(kvn, wukv_ref[...])
    o = MLA_Q_RANK + MLA_KV_RANK
    kpe = ha[:, o:o + LANES] * ck_ref[...] + ha[:, o + LANES:o + 2 * LANES] * sk_ref[...]
    k_width = MLA_HEADS * MLA_SLOT
    km_ref[...] = (kv[:, :k_width] + _dot(kpe.astype(BF16), pk_ref[...])).astype(BF16)
    vm_ref[...] = kv[:, k_width:].astype(BF16)

    hb = _dot(xb, w2_ref[...])
    P = RET_PAD
    rq = hb[:, 0:P] * cr_ref[...] + hb[:, P:2 * P] * sr_ref[...]
    rk = (hb[:, 2 * P:3 * P] * cr_ref[...] + hb[:, 3 * P:4 * P] * sr_ref[...]) * (RET_DK ** -0.5)
    rq_ref[...] = rq[:, :W_RET].astype(BF16)
    rk_ref[...] = rk[:, :W_RET].astype(BF16)
    rv_ref[...] = hb[:, 4 * P:4 * P + W_RET].astype(BF16)
    rg_ref[...] = hb[:, 5 * P:5 * P + W_RET]

    hc = _dot(xb, w3_ref[...])
    P = MOBA_PAD
    mq_ref[...] = (hc[:, :W_MOBA] * (MOBA_DH ** -0.5)).astype(BF16)
    mk = hc[:, P:P + W_MOBA]
    mk_ref[...] = mk.astype(BF16)
    mv_ref[...] = hc[:, 2 * P:2 * P + W_MOBA].astype(BF16)
    for bi in range(blocks_per_tile):
        kmean_ref[bi] = jnp.mean(mk[bi * MOBA_BLOCK:(bi + 1) * MOBA_BLOCK], axis=0, keepdims=True)


def _in_proj(x, ln, weights, tables, *, seq, tm):
    T, D = x.shape
    n_tiles = T // tm
    tiles_per_seq = seq // tm
    bpt = tm // MOBA_BLOCK
    apply_ln = ln is not None
    row = lambda i: (i, 0)
    const = lambda i: (0, 0)
    pos = lambda i: (i % tiles_per_seq, 0)

    in_specs, args = [pl.BlockSpec((tm, D), row)], [x]
    if apply_ln:
        in_specs += [pl.BlockSpec((1, D), const)] * 2
        args += list(ln)
    for w in weights:
        in_specs.append(pl.BlockSpec(w.shape, const))
        args.append(w)
    for t in tables:
        in_specs.append(pl.BlockSpec((tm, t.shape[1]), pos))
        args.append(t)

    def out(width, dtype):
        return jax.ShapeDtypeStruct((T, width), dtype), pl.BlockSpec((tm, width), row)

    outs = []
    if apply_ln:
        outs.append(out(D, F32))
    outs += [out(MLA_HEADS * MLA_SLOT, BF16), out(MLA_HEADS * MLA_SLOT, BF16), out(W_MLA_OUT, BF16),
             out(W_RET, BF16), out(W_RET, BF16), out(W_RET, BF16), out(W_RET, F32),
             out(W_MOBA, BF16), out(W_MOBA, BF16), out(W_MOBA, BF16)]
    outs.append((jax.ShapeDtypeStruct((T // MOBA_BLOCK, 1, W_MOBA), F32),
                 pl.BlockSpec((bpt, 1, W_MOBA), lambda i: (i, 0, 0))))
    return pl.pallas_call(
        functools.partial(_in_proj_kernel, apply_ln=apply_ln, blocks_per_tile=bpt),
        grid=(n_tiles,),
        in_specs=in_specs,
        out_specs=[o[1] for o in outs],
        out_shape=[o[0] for o in outs],
        compiler_params=pltpu.CompilerParams(dimension_semantics=("parallel",)),
        name="in_proj_ln" if apply_ln else "in_proj",
    )(*args)


def _softmax_step(s, v, m, l, acc):
    m_new = jnp.maximum(m, jnp.max(s, -1, keepdims=True))
    p = jnp.exp(s - m_new)
    alpha = jnp.exp(m - m_new)
    l = alpha * l + jnp.sum(p, -1, keepdims=True)
    acc = alpha * acc + _dot(p.astype(BF16), v)
    return m_new, l, acc


def _mla_attn_kernel(q_ref, k_ref, v_ref, o_ref, *, tq):
    i = pl.program_id(1)
    rows = lax.broadcasted_iota(jnp.int32, (tq, tq), 0)
    cols = lax.broadcasted_iota(jnp.int32, (tq, tq), 1)
    causal = cols <= rows
    for h in range(MLA_HEADS):
        q = q_ref[:, MLA_SLOT * h:MLA_SLOT * (h + 1)]

        def block(j, carry, diagonal, h=h, q=q):
            start = pl.multiple_of(j * tq, tq)
            k = k_ref[pl.ds(start, tq), MLA_SLOT * h:MLA_SLOT * (h + 1)]
            v = v_ref[pl.ds(start, tq), MLA_V * h:MLA_V * (h + 1)]
            s = _dot_nt(q, k)
            if diagonal:
                s = jnp.where(causal, s, NEG)
            return _softmax_step(s, v, *carry)

        init = (jnp.full((tq, 1), NEG, F32), jnp.zeros((tq, 1), F32), jnp.zeros((tq, MLA_V), F32))
        carry = block(i, init, True)
        m, l, acc = lax.fori_loop(0, i, functools.partial(block, diagonal=False), carry)
        o_ref[:, MLA_V * h:MLA_V * (h + 1)] = acc / l


def _mla_attention(q, k, v, *, batch, seq, tq):
    T = q.shape[0]
    nq = seq // tq
    return pl.pallas_call(
        functools.partial(_mla_attn_kernel, tq=tq),
        grid=(batch, nq),
        in_specs=[pl.BlockSpec((tq, q.shape[1]), lambda b, i: (b * nq + i, 0)),
                  pl.BlockSpec((seq, k.shape[1]), lambda b, i: (b, 0)),
                  pl.BlockSpec((seq, v.shape[1]), lambda b, i: (b, 0))],
        out_specs=pl.BlockSpec((tq, W_MLA_OUT), lambda b, i: (b * nq + i, 0)),
        out_shape=jax.ShapeDtypeStruct((T, W_MLA_OUT), F32),
        compiler_params=pltpu.CompilerParams(dimension_semantics=("parallel", "arbitrary")),
        name="mla_attention",
    )(q, k, v)


def _moba_kernel(q_ref, k_ref, v_ref, kmean_ref, o_ref, bias_ref, *, n_blocks, slopes):
    tq = MOBA_BLOCK
    i = pl.program_id(1)
    rows = lax.broadcasted_iota(jnp.int32, (tq, tq), 0)
    cols = lax.broadcasted_iota(jnp.int32, (tq, tq), 1)
    causal = cols <= rows
    key_off = lax.broadcasted_iota(jnp.int32, (1, tq), 1).astype(F32)
    blk_id = lax.broadcasted_iota(jnp.int32, (n_blocks, tq), 0)
    past = blk_id < i
    for h in range(MOBA_HEADS):
        hs = slice(MOBA_DH * h, MOBA_DH * (h + 1))
        q = q_ref[:, hs]

        km = kmean_ref[:, 0, hs]
        km_hi = km.astype(BF16)
        km_lo = (km - km_hi.astype(F32)).astype(BF16)
        gate = _dot_nt(km_hi, q) + _dot_nt(km_lo, q)
        gate = jnp.where(past, gate, -jnp.inf)
        rank = jnp.zeros((n_blocks, tq), F32)
        for mb in range(n_blocks):
            gm = gate[mb:mb + 1, :]
            beats = (gm > gate) | ((gm == gate) & (blk_id > mb))
            rank = rank + jnp.where(beats, 1.0, 0.0)
        chosen = past & (rank < MOBA_TOPK)
        sel_t = jnp.transpose(jnp.where(chosen, 0.0, NEG))
        for nb in range(n_blocks):
            bias_ref[nb] = jnp.broadcast_to(sel_t[:, nb:nb + 1], (tq, tq))

        def block(j, carry, diagonal, h=h, hs=hs, q=q):
            start = pl.multiple_of(j * tq, tq)
            k = k_ref[pl.ds(start, tq), hs]
            v = v_ref[pl.ds(start, tq), hs]
            key_bias = slopes[h] * (key_off + ((j - i) * tq).astype(F32))
            s = _dot_nt(q, k) + key_bias
            if diagonal:
                s = jnp.where(causal, s, NEG)
            else:
                s = s + bias_ref[j]
            return _softmax_step(s, v, *carry)

        init = (jnp.full((tq, 1), NEG, F32), jnp.zeros((tq, 1), F32), jnp.zeros((tq, MOBA_DH), F32))
        carry = block(i, init, True)
        m, l, acc = lax.fori_loop(0, i, functools.partial(block, diagonal=False), carry)
        o_ref[:, hs] = acc / l


def _moba_attention(q, k, v, kmean, *, batch, seq):
    T = q.shape[0]
    tq = MOBA_BLOCK
    nq = seq // tq
    slopes = tuple(float(2.0 ** (-8.0 * (h + 1.0) / MOBA_HEADS)) for h in range(MOBA_HEADS))
    return pl.pallas_call(
        functools.partial(_moba_kernel, n_blocks=nq, slopes=slopes),
        grid=(batch, nq),
        in_specs=[pl.BlockSpec((tq, W_MOBA), lambda b, i: (b * nq + i, 0)),
                  pl.BlockSpec((seq, W_MOBA), lambda b, i: (b, 0)),
                  pl.BlockSpec((seq, W_MOBA), lambda b, i: (b, 0)),
                  pl.BlockSpec((nq, 1, W_MOBA), lambda b, i: (b, 0, 0))],
        out_specs=pl.BlockSpec((tq, W_MOBA), lambda b, i: (b * nq + i, 0)),
        out_shape=jax.ShapeDtypeStruct((T, W_MOBA), F32),
        scratch_shapes=[pltpu.VMEM((nq, tq, tq), F32)],
        compiler_params=pltpu.CompilerParams(dimension_semantics=("parallel", "arbitrary")),
        name="moba_attention",
    )(q, k, v, kmean)


def _retention_kernel(q_ref, k_ref, v_ref, g_ref, din_ref, qdec_ref, kdec_ref, o_ref, state_ref, *, chunk_decay):
    @pl.when(pl.program_id(1) == 0)
    def _():
        state_ref[...] = jnp.zeros_like(state_ref)

    for h in range(RET_HEADS):
        hs = slice(RET_DK * h, RET_DK * (h + 1))
        q = q_ref[:, hs]
        k = k_ref[:, hs]
        v = v_ref[:, hs]
        intra = _dot_nt(q, k) * din_ref[h]
        o = _dot(intra.astype(BF16), v)
        state = state_ref[h]
        o = o + _dot(q, state.astype(BF16)) * qdec_ref[:, hs]
        kd = (k.astype(F32) * kdec_ref[:, hs]).astype(BF16)
        kv = lax.dot_general(kd, v, (((0,), (0,)), ((), ())), preferred_element_type=F32)
        state_ref[h] = state * chunk_decay[h] + kv
        mu = jnp.mean(o, -1, keepdims=True)
        oc = o - mu
        var = jnp.mean(oc * oc, -1, keepdims=True)
        g = g_ref[:, hs]
        o_ref[:, hs] = (_silu(g) * (oc * lax.rsqrt(var + RMS_EPS))).astype(BF16)


def _retention(q, k, v, g, *, batch, seq):
    T = q.shape[0]
    C = RET_CHUNK
    nc = seq // C
    log_gamma = np.log(1.0 - 2.0 ** (-5.0 - np.arange(RET_HEADS, dtype=np.float64)))
    idx = np.arange(C, dtype=np.float64)
    diff = idx[:, None] - idx[None, :]
    decay_in = np.where(diff >= 0, np.exp(log_gamma[:, None, None] * np.maximum(diff, 0.0)), 0.0)
    q_decay = np.repeat(np.exp(log_gamma[None, :] * (idx[:, None] + 1.0)), RET_DK, axis=1)
    k_decay = np.repeat(np.exp(log_gamma[None, :] * (C - 1.0 - idx[:, None])), RET_DK, axis=1)
    chunk_decay = tuple(float(c) for c in np.exp(log_gamma * C))
    tile = lambda b, c: (b * nc + c, 0)
    return pl.pallas_call(
        functools.partial(_retention_kernel, chunk_decay=chunk_decay),
        grid=(batch, nc),
        in_specs=[pl.BlockSpec((C, W_RET), tile)] * 4
        + [pl.BlockSpec((RET_HEADS, C, C), lambda b, c: (0, 0, 0)),
           pl.BlockSpec((C, W_RET), lambda b, c: (0, 0)),
           pl.BlockSpec((C, W_RET), lambda b, c: (0, 0))],
        out_specs=pl.BlockSpec((C, W_RET), tile),
        out_shape=jax.ShapeDtypeStruct((T, W_RET), BF16),
        scratch_shapes=[pltpu.VMEM((RET_HEADS, RET_DK, RET_DV), F32)],
        compiler_params=pltpu.CompilerParams(dimension_semantics=("parallel", "arbitrary")),
        name="retention",
    )(q, k, v, g, jnp.asarray(decay_in, F32), jnp.asarray(q_decay, F32), jnp.asarray(k_decay, F32))


def _out_proj_kernel(oa_ref, ob_ref, oc_ref, x_ref, ba_ref, bc_ref, wa_ref, wb_ref, wc_ref, g_ref, b_ref,
                     y_ref, *, alpha):
    na = _rms_norm(oa_ref[...], ba_ref[...]).astype(BF16)
    nc = _rms_norm(oc_ref[...], bc_ref[...]).astype(BF16)
    mix = _dot(na, wa_ref[...]) + _dot(ob_ref[...], wb_ref[...]) + _dot(nc, wc_ref[...])
    y_ref[...] = _layer_norm(alpha * x_ref[...] + mix, g_ref[...], b_ref[...])


def _out_proj(oa, ob, oc, x, beta_a, beta_c, wa, wb, wc, g, b, *, alpha, tm):
    T, D = x.shape
    row = lambda i: (i, 0)
    const = lambda i: (0, 0)
    full = lambda a: pl.BlockSpec(a.shape, const)
    return pl.pallas_call(
        functools.partial(_out_proj_kernel, alpha=alpha),
        grid=(T // tm,),
        in_specs=[pl.BlockSpec((tm, oa.shape[1]), row), pl.BlockSpec((tm, ob.shape[1]), row),
                  pl.BlockSpec((tm, oc.shape[1]), row), pl.BlockSpec((tm, D), row),
                  full(beta_a), full(beta_c), full(wa), full(wb), full(wc), full(g), full(b)],
        out_specs=pl.BlockSpec((tm, D), row),
        out_shape=jax.ShapeDtypeStruct((T, D), F32),
        compiler_params=pltpu.CompilerParams(dimension_semantics=("parallel",)),
        name="out_proj",
    )(oa, ob, oc, x, beta_a, beta_c, wa, wb, wc, g, b)


def _ffn_kernel(x_ref, wg_ref, wu_ref, wd_ref, g_ref, b_ref, y_ref, xb_ref, acc_ref, *, alpha):
    j = pl.program_id(1)

    @pl.when(j == 0)
    def _():
        xb_ref[...] = x_ref[...].astype(BF16)
        acc_ref[...] = jnp.zeros_like(acc_ref)

    xb = xb_ref[...]
    h = _silu(_dot(xb, wg_ref[...])) * _dot(xb, wu_ref[...])
    acc_ref[...] += _dot(h.astype(BF16), wd_ref[...])

    @pl.when(j == pl.num_programs(1) - 1)
    def _():
        y_ref[...] = _layer_norm(alpha * x_ref[...] + acc_ref[...], g_ref[...], b_ref[...])


def _ffn(x, wg, wu, wd, g, b, *, alpha, tm, tf):
    T, D = x.shape
    F = wg.shape[1]
    return pl.pallas_call(
        functools.partial(_ffn_kernel, alpha=alpha),
        grid=(T // tm, F // tf),
        in_specs=[pl.BlockSpec((tm, D), lambda i, j: (i, 0)),
                  pl.BlockSpec((D, tf), lambda i, j: (0, j)),
                  pl.BlockSpec((D, tf), lambda i, j: (0, j)),
                  pl.BlockSpec((tf, D), lambda i, j: (j, 0)),
                  pl.BlockSpec((1, D), lambda i, j: (0, 0)),
                  pl.BlockSpec((1, D), lambda i, j: (0, 0))],
        out_specs=pl.BlockSpec((tm, D), lambda i, j: (i, 0)),
        out_shape=jax.ShapeDtypeStruct((T, D), F32),
        scratch_shapes=[pltpu.VMEM((tm, D), BF16), pltpu.VMEM((tm, D), F32)],
        compiler_params=pltpu.CompilerParams(dimension_semantics=("parallel", "arbitrary")),
        name="ffn",
    )(x, wg, wu, wd, g, b)


def _router_kernel(x_ref, rhi_ref, rlo_ref, gates_ref):
    x = x_ref[...]
    x_hi = x.astype(BF16)
    x_lo = (x - x_hi.astype(F32)).astype(BF16)
    logits = _dot(x_hi, rhi_ref[...]) + _dot(x_lo, rhi_ref[...]) + _dot(x_hi, rlo_ref[...])
    lane = lax.broadcasted_iota(jnp.int32, logits.shape, 1)
    logits = jnp.where(lane < N_EXPERTS, logits, -jnp.inf)
    m1 = jnp.max(logits, -1, keepdims=True)
    i1 = jnp.min(jnp.where(logits == m1, lane, LANES), -1, keepdims=True)
    rest = jnp.where(lane == i1, -jnp.inf, logits)
    m2 = jnp.max(rest, -1, keepdims=True)
    i2 = jnp.min(jnp.where(rest == m2, lane, LANES), -1, keepdims=True)
    e2 = jnp.exp(m2 - m1)
    w1 = 1.0 / (1.0 + e2)
    w2 = e2 / (1.0 + e2)
    gates_ref[...] = jnp.where(lane == i1, w1, 0.0) + jnp.where(lane == i2, w2, 0.0)


def _router(x, r_hi, r_lo, *, tm):
    T, D = x.shape
    return pl.pallas_call(
        _router_kernel,
        grid=(T // tm,),
        in_specs=[pl.BlockSpec((tm, D), lambda i: (i, 0)),
                  pl.BlockSpec((D, LANES), lambda i: (0, 0)),
                  pl.BlockSpec((D, LANES), lambda i: (0, 0))],
        out_specs=pl.BlockSpec((tm, LANES), lambda i: (i, 0)),
        out_shape=jax.ShapeDtypeStruct((T, LANES), F32),
        compiler_params=pltpu.CompilerParams(dimension_semantics=("parallel",)),
        name="router",
    )(x, r_hi, r_lo)


def _moe_kernel(x_ref, gates_ref, wg_ref, wu_ref, wd_ref, g_ref, b_ref, y_ref, xb_ref, acc_ref, gate_ref, *, alpha):
    e = pl.program_id(1)
    j = pl.program_id(2)

    @pl.when((e == 0) & (j == 0))
    def _():
        xb_ref[...] = x_ref[...].astype(BF16)
        acc_ref[...] = jnp.zeros_like(acc_ref)

    @pl.when(j == 0)
    def _():
        gates = gates_ref[...]
        lane = lax.broadcasted_iota(jnp.int32, gates.shape, 1)
        gate_ref[...] = jnp.sum(jnp.where(lane == e, gates, 0.0), -1, keepdims=True)

    xb = xb_ref[...]
    h = _silu(_dot(xb, wg_ref[...])) * _dot(xb, wu_ref[...]) * gate_ref[...]
    acc_ref[...] += _dot(h.astype(BF16), wd_ref[...])

    @pl.when((e == pl.num_programs(1) - 1) & (j == pl.num_programs(2) - 1))
    def _():
        y_ref[...] = _layer_norm(alpha * x_ref[...] + acc_ref[...], g_ref[...], b_ref[...])


def _moe(x, gates, wg, wu, wd, g, b, *, alpha, tm, tf):
    T, D = x.shape
    E, _, F = wg.shape
    return pl.pallas_call(
        functools.partial(_moe_kernel, alpha=alpha),
        grid=(T // tm, E, F // tf),
        in_specs=[pl.BlockSpec((tm, D), lambda i, e, j: (i, 0)),
                  pl.BlockSpec((tm, LANES), lambda i, e, j: (i, 0)),
                  pl.BlockSpec((None, D, tf), lambda i, e, j: (e, 0, j)),
                  pl.BlockSpec((None, D, tf), lambda i, e, j: (e, 0, j)),
                  pl.BlockSpec((None, tf, D), lambda i, e, j: (e, j, 0)),
                  pl.BlockSpec((1, D), lambda i, e, j: (0, 0)),
                  pl.BlockSpec((1, D), lambda i, e, j: (0, 0))],
        out_specs=pl.BlockSpec((tm, D), lambda i, e, j: (i, 0)),
        out_shape=jax.ShapeDtypeStruct((T, D), F32),
        scratch_shapes=[pltpu.VMEM((tm, D), BF16), pltpu.VMEM((tm, D), F32), pltpu.VMEM((tm, 1), F32)],
        compiler_params=pltpu.CompilerParams(dimension_semantics=("parallel", "arbitrary", "arbitrary")),
        name="moe",
    )(x, gates, wg, wu, wd, g, b)


def _pad_cols(w, width):
    return jnp.pad(w, ((0, 0), (0, width - w.shape[1])))


def _swap_halves(w, head_dim):
    d = w.shape[1]
    idx = np.arange(d).reshape(d // head_dim, 2, head_dim // 2)[:, ::-1, :].reshape(d)
    return w[:, idx]


def _prep_in_weights(w_in, w_uq, w_ukv):
    sizes = (MLA_Q_RANK, MLA_KV_RANK, MLA_ROPE, W_RET, W_RET, W_RET, W_RET, W_MOBA, W_MOBA, W_MOBA)
    offs = np.cumsum((0,) + sizes)
    c_q, c_kv, k_rope, rq, rk, rv, rg, mq, mk, mv = (w_in[:, offs[n]:offs[n + 1]] for n in range(len(sizes)))
    w1 = jnp.concatenate([c_q, c_kv, _pad_cols(k_rope, LANES), _pad_cols(_swap_halves(k_rope, MLA_ROPE), LANES)], 1)
    w2 = jnp.concatenate([_pad_cols(w, RET_PAD) for w in
                          (rq, _swap_halves(rq, RET_DK), rk, _swap_halves(rk, RET_DK), rv, rg)], 1)
    w3 = jnp.concatenate([_pad_cols(w, MOBA_PAD) for w in (mq, mk, mv)], 1)

    dq = MLA_NOPE + MLA_ROPE
    uq = w_uq.reshape(MLA_Q_RANK, MLA_HEADS, dq)
    uq_rope_sw = _swap_halves(uq[:, :, MLA_NOPE:].reshape(MLA_Q_RANK, -1), MLA_ROPE).reshape(MLA_Q_RANK, MLA_HEADS, MLA_ROPE)
    zpad = jnp.zeros((MLA_Q_RANK, MLA_HEADS, MLA_SLOT - dq), w_uq.dtype)
    wuq = jnp.concatenate([uq, zpad], 2).reshape(MLA_Q_RANK, -1)
    wuq_sw = jnp.concatenate([jnp.zeros_like(uq[:, :, :MLA_NOPE]), uq_rope_sw, zpad], 2).reshape(MLA_Q_RANK, -1)
    ukv = w_ukv.reshape(MLA_KV_RANK, MLA_HEADS, MLA_NOPE + MLA_V)
    k_part = jnp.concatenate([ukv[:, :, :MLA_NOPE], jnp.zeros((MLA_KV_RANK, MLA_HEADS, MLA_SLOT - MLA_NOPE), w_ukv.dtype)], 2)
    wukv = jnp.concatenate([k_part.reshape(MLA_KV_RANK, -1), ukv[:, :, MLA_NOPE:].reshape(MLA_KV_RANK, -1)], 1)
    return tuple(w.astype(BF16) for w in (w1, w2, w3)), tuple(w.astype(BF16) for w in (wuq, wuq_sw, wukv))


def _rope_key_placement():
    p = np.zeros((LANES, MLA_HEADS * MLA_SLOT), np.float32)
    for h in range(MLA_HEADS):
        for j in range(MLA_ROPE):
            p[j, h * MLA_SLOT + MLA_NOPE + j] = 1.0
    return jnp.asarray(p, BF16)


def _rope_tables(seq):
    def cos_sin(dim):
        pos = jnp.arange(seq, dtype=F32)
        inv = ROPE_THETA ** (-jnp.arange(0, dim, 2, dtype=F32) / dim)
        ang = pos[:, None] * inv[None, :]
        c, s = jnp.cos(ang), jnp.sin(ang)
        return jnp.concatenate([c, c], 1), jnp.concatenate([-s, s], 1)

    ca, sa = cos_sin(MLA_ROPE)
    cb, sb = cos_sin(RET_DK)
    scale = (MLA_NOPE + MLA_ROPE) ** -0.5
    ones = jnp.ones((seq, MLA_NOPE), F32)
    zeros_n = jnp.zeros((seq, MLA_NOPE), F32)
    zpad = jnp.zeros((seq, MLA_SLOT - MLA_NOPE - MLA_ROPE), F32)
    cq = jnp.tile(jnp.concatenate([ones, ca, zpad], 1), (1, MLA_HEADS)) * scale
    sq = jnp.tile(jnp.concatenate([zeros_n, sa, zpad], 1), (1, MLA_HEADS)) * scale
    ck = _pad_cols(ca, LANES)
    sk = _pad_cols(sa, LANES)
    cr = _pad_cols(jnp.tile(cb, (1, RET_HEADS)), RET_PAD)
    sr = _pad_cols(jnp.tile(sb, (1, RET_HEADS)), RET_PAD)
    return cq, sq, ck, sk, cr, sr


def kernel(x, ln_emb_g, ln_emb_b, w_in, q_norm_g, kv_norm_g, w_uq, w_ukv, beta_mla, beta_moba, w_o, ln1_g, ln1_b,
           ffn_w_gate, ffn_w_up, ffn_w_down, router, exp_w_gate, exp_w_up, exp_w_down, ln2_g, ln2_b):
    B, S, D = x.shape
    depth = w_in.shape[0]
    alpha = float((2 * depth) ** 0.25)
    assert S % MOBA_BLOCK == 0 and S % RET_CHUNK == 0
    tm = 512 if S % 512 == 0 else MOBA_BLOCK
    row = lambda v: v.reshape(1, -1).astype(F32)

    tables = _rope_tables(S)
    pk = _rope_key_placement()
    xs = x.reshape(B * S, D)
    for l in range(depth):
        w123, wmla = _prep_in_weights(w_in[l], w_uq[l], w_ukv[l])
        weights = (*w123, row(q_norm_g[l]), row(kv_norm_g[l]), *wmla, pk)
        ln = (row(ln_emb_g), row(ln_emb_b)) if l == 0 else None
        outs = _in_proj(xs, ln, weights, tables, seq=S, tm=tm)
        if l == 0:
            xs, *outs = outs
        qm, km, vm, rq, rk, rv, rg, mq, mk, mv, kmean = outs
        o_a = _mla_attention(qm, km, vm, batch=B, seq=S, tq=MOBA_BLOCK)
        o_b = _retention(rq, rk, rv, rg, batch=B, seq=S)
        o_c = _moba_attention(mq, mk, mv, kmean, batch=B, seq=S)
        wo = w_o[l].astype(BF16)
        xs = _out_proj(o_a, o_b, o_c, xs, row(beta_mla[l]), row(beta_moba[l]),
                       wo[:W_MLA_OUT], wo[W_MLA_OUT:W_MLA_OUT + W_RET], wo[W_MLA_OUT + W_RET:],
                       row(ln1_g[l]), row(ln1_b[l]), alpha=alpha, tm=tm)
        j = l // 2
        if l % 2 == 0:
            xs = _ffn(xs, ffn_w_gate[j].astype(BF16), ffn_w_up[j].astype(BF16), ffn_w_down[j].astype(BF16),
                      row(ln2_g[l]), row(ln2_b[l]), alpha=alpha, tm=tm, tf=256)
        else:
            r = router[j].astype(F32)
            r_hi = r.astype(BF16)
            r_lo = (r - r_hi.astype(F32)).astype(BF16)
            gates = _router(xs, _pad_cols(r_hi, LANES), _pad_cols(r_lo, LANES), tm=tm)
            xs = _moe(xs, gates, exp_w_gate[j].astype(BF16), exp_w_up[j].astype(BF16), exp_w_down[j].astype(BF16),
                      row(ln2_g[l]), row(ln2_b[l]), alpha=alpha, tm=tm, tf=512)
    return xs.reshape(B, S, D)
```

```python
import functools

import numpy as np
import jax
import jax.numpy as jnp
from jax import lax
from jax.experimental import pallas as pl
from jax.experimental.pallas import tpu as pltpu

F32 = jnp.float32
BF16 = jnp.bfloat16

MLA_HEADS, MLA_NOPE, MLA_ROPE, MLA_V = 6, 64, 32, 64
MLA_Q_RANK, MLA_KV_RANK = 256, 128
RET_HEADS, RET_DK, RET_DV, RET_CHUNK = 5, 64, 64, 128
MOBA_HEADS, MOBA_DH, MOBA_BLOCK, MOBA_TOPK = 5, 64, 256, 3
ROPE_THETA = 10000.0
LN_EPS = 1e-5
RMS_EPS = 1e-6
N_EXPERTS = 8
W_MLA_OUT = MLA_HEADS * MLA_V
W_RET = RET_HEADS * RET_DK
W_MOBA = MOBA_HEADS * MOBA_DH

LANES = 128
MLA_SLOT = 128
NEG = -1e30


def _pad_to(n, m):
    return -(-n // m) * m


RET_PAD = _pad_to(W_RET, LANES)
MOBA_PAD = _pad_to(W_MOBA, LANES)
MOBA_SLOT = 128
MOBA_SLOTS = MOBA_HEADS * MOBA_SLOT
LOG2E = 1.4426950408889634


def _layer_norm(x, g, b):
    mu = jnp.mean(x, -1, keepdims=True)
    xc = x - mu
    var = jnp.mean(xc * xc, -1, keepdims=True)
    return xc * lax.rsqrt(var + LN_EPS) * g + b


def _rms_norm(x, g):
    return x * lax.rsqrt(jnp.mean(x * x, -1, keepdims=True) + RMS_EPS) * g


def _silu(x):
    return x / (1.0 + jnp.exp(-x))


def _dot(a, b):
    return jnp.dot(a, b, preferred_element_type=F32)


def _dot_nt(a, b):
    return lax.dot_general(a, b, (((1,), (1,)), ((), ())), preferred_element_type=F32)


def _in_proj_kernel(*refs, apply_ln, blocks_per_tile):
    if apply_ln:
        x_ref, lng_ref, lnb_ref, *refs = refs
    else:
        x_ref, *refs = refs
    (w1_ref, w2_ref, w3_ref, qg_ref, kvg_ref, wuq_ref, wuqs_ref, wukv_ref, pk_ref, mqa_ref, mka_ref,
     cq_ref, sq_ref, ck_ref, sk_ref, cr_ref, sr_ref, *outs) = refs
    if apply_ln:
        x0_ref, *outs = outs
    (qm_ref, km_ref, vm_ref, rq_ref, rk_ref, rv_ref, rg_ref, mq_ref, mk_ref, mv_ref, kmean_ref) = outs

    x = x_ref[...]
    if apply_ln:
        x = _layer_norm(x, lng_ref[...], lnb_ref[...])
        x0_ref[...] = x
    xb = x.astype(BF16)

    ha = _dot(xb, w1_ref[...])
    qn = _rms_norm(ha[:, :MLA_Q_RANK], qg_ref[...]).astype(BF16)
    q = _dot(qn, wuq_ref[...]) * cq_ref[...] + _dot(qn, wuqs_ref[...]) * sq_ref[...]
    qm_ref[...] = q.astype(BF16)
    kvn = _rms_norm(ha[:, MLA_Q_RANK:MLA_Q_RANK + MLA_KV_RANK], kvg_ref[...]).astype(BF16)
    kv = _dot(kvn, wukv_ref[...])
    o = MLA_Q_RANK + MLA_KV_RANK
    kpe = ha[:, o:o + LANES] * ck_ref[...] + ha[:, o + LANES:o + 2 * LANES] * sk_ref[...]
    k_width = MLA_HEADS * MLA_SLOT
    km_ref[...] = (kv[:, :k_width] + _dot(kpe.astype(BF16), pk_ref[...])).astype(BF16)
    vm_ref[...] = kv[:, k_width:].T.astype(BF16)

    hb = _dot(xb, w2_ref[...])
    P = RET_PAD
    rq = hb[:, 0:P] * cr_ref[...] + hb[:, P:2 * P] * sr_ref[...]
    rk = (hb[:, 2 * P:3 * P] * cr_ref[...] + hb[:, 3 * P:4 * P] * sr_ref[...]) * (RET_DK ** -0.5)
    rq_ref[...] = rq[:, :W_RET].astype(BF16)
    rk_ref[...] = rk[:, :W_RET].astype(BF16)
    rv_ref[...] = hb[:, 4 * P:4 * P + W_RET].astype(BF16)
    rg_ref[...] = hb[:, 5 * P:5 * P + W_RET]

    hc = _dot(xb, w3_ref[...])
    P = MOBA_SLOTS
    mq_ref[...] = (hc[:, :P] * (MOBA_DH ** -0.5 * LOG2E) + mqa_ref[...]).astype(BF16)
    mk = hc[:, P:2 * P]
    mk_ref[...] = (mk + mka_ref[...]).astype(BF16)
    mv_ref[...] = hc[:, 2 * P:].T.astype(BF16)
    for bi in range(blocks_per_tile):
        kmean_ref[bi] = jnp.mean(mk[bi * MOBA_BLOCK:(bi + 1) * MOBA_BLOCK], axis=0, keepdims=True)


def _in_proj(x, ln, weights, tables, *, seq, tm):
    T, D = x.shape
    n_tiles = T // tm
    tiles_per_seq = seq // tm
    bpt = tm // MOBA_BLOCK
    apply_ln = ln is not None
    row = lambda i: (i, 0)
    const = lambda i: (0, 0)
    pos = lambda i: (i % tiles_per_seq, 0)

    in_specs, args = [pl.BlockSpec((tm, D), row)], [x]
    if apply_ln:
        in_specs += [pl.BlockSpec((1, D), const)] * 2
        args += list(ln)
    for w in weights:
        in_specs.append(pl.BlockSpec(w.shape, const))
        args.append(w)
    for t in tables:
        in_specs.append(pl.BlockSpec((tm, t.shape[1]), pos))
        args.append(t)

    def out(width, dtype):
        return jax.ShapeDtypeStruct((T, width), dtype), pl.BlockSpec((tm, width), row)

    outs = []
    if apply_ln:
        outs.append(out(D, F32))
    def out_t(height, dtype):
        return jax.ShapeDtypeStruct((height, T), dtype), pl.BlockSpec((height, tm), lambda i: (0, i))

    outs += [out(MLA_HEADS * MLA_SLOT, BF16), out(MLA_HEADS * MLA_SLOT, BF16), out_t(W_MLA_OUT, BF16),
             out(W_RET, BF16), out(W_RET, BF16), out(W_RET, BF16), out(W_RET, F32),
             out(MOBA_SLOTS, BF16), out(MOBA_SLOTS, BF16), out_t(MOBA_PAD, BF16)]
    outs.append((jax.ShapeDtypeStruct((T // MOBA_BLOCK, 1, MOBA_SLOTS), F32),
                 pl.BlockSpec((bpt, 1, MOBA_SLOTS), lambda i: (i, 0, 0))))
    return pl.pallas_call(
        functools.partial(_in_proj_kernel, apply_ln=apply_ln, blocks_per_tile=bpt),
        grid=(n_tiles,),
        in_specs=in_specs,
        out_specs=[o[1] for o in outs],
        out_shape=[o[0] for o in outs],
        compiler_params=pltpu.CompilerParams(dimension_semantics=("parallel",)),
        name="in_proj_ln" if apply_ln else "in_proj",
    )(*args)


SUM_ROWS = 16


def _softmax_step(s, vt, m, acc, query_bias=None):
    smax = jnp.max(s, 0, keepdims=True)
    if query_bias is not None:
        smax = smax + query_bias
    m_new = jnp.maximum(m, smax)
    shift = m_new if query_bias is None else m_new - query_bias
    p = jnp.exp2(s - shift).astype(BF16)
    vt_ones = jnp.concatenate([vt, jnp.ones((SUM_ROWS, vt.shape[1]), BF16)], axis=0)
    acc = jnp.exp2(m - m_new) * acc + _dot(vt_ones, p)
    return m_new, acc


def _softmax_init(n_heads, dv, tq):
    return tuple((jnp.full((1, tq), NEG, F32), jnp.zeros((dv + SUM_ROWS, tq), F32)) for _ in range(n_heads))


def _softmax_finish(carry, dv):
    return [acc[:dv] / acc[dv:dv + 1] for _, acc in carry]


def _all_heads(n_heads, score, update, carry):
    scores = [score(h) for h in range(n_heads)]
    return tuple(update(h, scores[h], carry[h]) for h in range(n_heads))


def _causal_mask(tq):
    keys = lax.broadcasted_iota(jnp.int32, (tq, tq), 0)
    queries = lax.broadcasted_iota(jnp.int32, (tq, tq), 1)
    return keys <= queries


def _mla_attn_kernel(q_ref, k_ref, vt_ref, o_ref, *, tq):
    i = pl.program_id(1)
    causal = _causal_mask(tq)
    slots = [slice(MLA_SLOT * h, MLA_SLOT * (h + 1)) for h in range(MLA_HEADS)]

    def block(j, carry, diagonal):
        start = pl.multiple_of(j * tq, tq)

        def score(h):
            return _dot_nt(k_ref[pl.ds(start, tq), slots[h]], q_ref[:, slots[h]])

        def update(h, s, state):
            s = jnp.where(causal, s, NEG) if diagonal else s
            return _softmax_step(s, vt_ref[MLA_V * h:MLA_V * (h + 1), pl.ds(start, tq)], *state)

        return _all_heads(MLA_HEADS, score, update, carry)

    carry = block(i, _softmax_init(MLA_HEADS, MLA_V, tq), True)
    carry = lax.fori_loop(0, i, functools.partial(block, diagonal=False), carry)
    out = jnp.concatenate(_softmax_finish(carry, MLA_V), axis=0)
    o_ref[...] = out.T


def _mla_attention(q, k, vt, *, batch, seq, tq):
    T = q.shape[0]
    nq = seq // tq
    return pl.pallas_call(
        functools.partial(_mla_attn_kernel, tq=tq),
        grid=(batch, nq),
        in_specs=[pl.BlockSpec((tq, q.shape[1]), lambda b, i: (b * nq + i, 0)),
                  pl.BlockSpec((seq, k.shape[1]), lambda b, i: (b, 0)),
                  pl.BlockSpec((vt.shape[0], seq), lambda b, i: (0, b))],
        out_specs=pl.BlockSpec((tq, W_MLA_OUT), lambda b, i: (b * nq + i, 0)),
        out_shape=jax.ShapeDtypeStruct((T, W_MLA_OUT), F32),
        compiler_params=pltpu.CompilerParams(dimension_semantics=("parallel", "arbitrary")),
        name="mla_attention",
    )(q, k, vt)


SLOPE_PARTS = 3


def _moba_slopes_log2():
    return tuple(float(np.sum(_slope_parts(h), dtype=np.float32)) for h in range(MOBA_HEADS))


def _slope_parts(h):
    rest = np.float32(2.0 ** (-8.0 * (h + 1.0) / MOBA_HEADS) * LOG2E)
    parts = []
    for _ in range(SLOPE_PARTS):
        piece = np.float32(rest.astype(BF16))
        parts.append(piece)
        rest = np.float32(rest - piece)
    return np.asarray(parts, np.float32)


def _moba_alibi_lanes(rows):
    q_add = np.zeros((1, MOBA_SLOTS), np.float32)
    k_add = np.zeros((rows, MOBA_SLOTS), np.float32)
    offset = (np.arange(rows) % MOBA_BLOCK).astype(np.float32)
    for h in range(MOBA_HEADS):
        lanes = slice(h * MOBA_SLOT + MOBA_DH, h * MOBA_SLOT + MOBA_DH + SLOPE_PARTS)
        q_add[0, lanes] = _slope_parts(h)
        k_add[:, lanes] = offset[:, None]
    return jnp.asarray(q_add), jnp.asarray(k_add)


def _moba_kernel(q_ref, k_ref, vt_ref, kmean_ref, o_ref, bias_ref, *, n_blocks, slopes):
    tq = MOBA_BLOCK
    i = pl.program_id(1)
    causal = _causal_mask(tq)
    blk_id = lax.broadcasted_iota(jnp.int32, (n_blocks, tq), 0)
    past = blk_id < i

    slots = [slice(MOBA_SLOT * h, MOBA_SLOT * (h + 1)) for h in range(MOBA_HEADS)]
    for h in range(MOBA_HEADS):
        q = q_ref[:, slots[h]]
        km = kmean_ref[:, 0, slots[h]]
        km_hi = km.astype(BF16)
        km_lo = (km - km_hi.astype(F32)).astype(BF16)
        gate = _dot_nt(km_hi, q) + _dot_nt(km_lo, q)
        gate = jnp.where(past, gate, -jnp.inf)
        rank = jnp.zeros((n_blocks, tq), F32)
        for mb in range(n_blocks):
            gm = gate[mb:mb + 1, :]
            beats = (gm > gate) | ((gm == gate) & (blk_id > mb))
            rank = rank + jnp.where(beats, 1.0, 0.0)
        chosen = past & (rank < MOBA_TOPK)
        bias = jnp.where(chosen, 0.0, NEG)
        for nb in range(n_blocks):
            bias_ref[h, nb] = bias[nb:nb + 1, :]

    def block(j, carry, diagonal):
        start = pl.multiple_of(j * tq, tq)

        def score(h):
            return _dot_nt(k_ref[pl.ds(start, tq), slots[h]], q_ref[:, slots[h]])

        def update(h, s, state):
            vt = vt_ref[MOBA_DH * h:MOBA_DH * (h + 1), pl.ds(start, tq)]
            if diagonal:
                return _softmax_step(jnp.where(causal, s, NEG), vt, *state)
            query_bias = bias_ref[h, j] + slopes[h] * ((j - i) * tq).astype(F32)
            return _softmax_step(s, vt, *state, query_bias=query_bias)

        return _all_heads(MOBA_HEADS, score, update, carry)

    carry = block(i, _softmax_init(MOBA_HEADS, MOBA_DH, tq), True)
    carry = lax.fori_loop(0, i, functools.partial(block, diagonal=False), carry)
    pad = jnp.zeros((MOBA_PAD - W_MOBA, tq), F32)
    out = jnp.concatenate(_softmax_finish(carry, MOBA_DH) + [pad], axis=0)
    o_ref[...] = out.T[:, :W_MOBA]


def _moba_attention(q, k, vt, kmean, *, batch, seq):
    T = q.shape[0]
    tq = MOBA_BLOCK
    nq = seq // tq
    slopes = _moba_slopes_log2()
    return pl.pallas_call(
        functools.partial(_moba_kernel, n_blocks=nq, slopes=slopes),
        grid=(batch, nq),
        in_specs=[pl.BlockSpec((tq, MOBA_SLOTS), lambda b, i: (b * nq + i, 0)),
                  pl.BlockSpec((seq, MOBA_SLOTS), lambda b, i: (b, 0)),
                  pl.BlockSpec((vt.shape[0], seq), lambda b, i: (0, b)),
                  pl.BlockSpec((nq, 1, MOBA_SLOTS), lambda b, i: (b, 0, 0))],
        out_specs=pl.BlockSpec((tq, W_MOBA), lambda b, i: (b * nq + i, 0)),
        out_shape=jax.ShapeDtypeStruct((T, W_MOBA), F32),
        scratch_shapes=[pltpu.VMEM((MOBA_HEADS, nq, 1, tq), F32)],
        compiler_params=pltpu.CompilerParams(dimension_semantics=("parallel", "arbitrary")),
        name="moba_attention",
    )(q, k, vt, kmean)


def _retention_kernel(q_ref, k_ref, v_ref, g_ref, din_ref, qdec_ref, kdec_ref, o_ref, state_ref, *, chunk_decay):
    @pl.when(pl.program_id(1) == 0)
    def _():
        state_ref[...] = jnp.zeros_like(state_ref)

    for h in range(RET_HEADS):
        hs = slice(RET_DK * h, RET_DK * (h + 1))
        q = q_ref[:, hs]
        k = k_ref[:, hs]
        v = v_ref[:, hs]
        intra = _dot_nt(q, k) * din_ref[h]
        o = _dot(intra.astype(BF16), v)
        state = state_ref[h]
        o = o + _dot(q, state.astype(BF16)) * qdec_ref[:, hs]
        kd = (k.astype(F32) * kdec_ref[:, hs]).astype(BF16)
        kv = lax.dot_general(kd, v, (((0,), (0,)), ((), ())), preferred_element_type=F32)
        state_ref[h] = state * chunk_decay[h] + kv
        mu = jnp.mean(o, -1, keepdims=True)
        oc = o - mu
        var = jnp.mean(oc * oc, -1, keepdims=True)
        g = g_ref[:, hs]
        o_ref[:, hs] = (_silu(g) * (oc * lax.rsqrt(var + RMS_EPS))).astype(BF16)


def _retention(q, k, v, g, *, batch, seq):
    T = q.shape[0]
    C = RET_CHUNK
    nc = seq // C
    log_gamma = np.log(1.0 - 2.0 ** (-5.0 - np.arange(RET_HEADS, dtype=np.float64)))
    idx = np.arange(C, dtype=np.float64)
    diff = idx[:, None] - idx[None, :]
    decay_in = np.where(diff >= 0, np.exp(log_gamma[:, None, None] * np.maximum(diff, 0.0)), 0.0)
    q_decay = np.repeat(np.exp(log_gamma[None, :] * (idx[:, None] + 1.0)), RET_DK, axis=1)
    k_decay = np.repeat(np.exp(log_gamma[None, :] * (C - 1.0 - idx[:, None])), RET_DK, axis=1)
    chunk_decay = tuple(float(c) for c in np.exp(log_gamma * C))
    tile = lambda b, c: (b * nc + c, 0)
    return pl.pallas_call(
        functools.partial(_retention_kernel, chunk_decay=chunk_decay),
        grid=(batch, nc),
        in_specs=[pl.BlockSpec((C, W_RET), tile)] * 4
        + [pl.BlockSpec((RET_HEADS, C, C), lambda b, c: (0, 0, 0)),
           pl.BlockSpec((C, W_RET), lambda b, c: (0, 0)),
           pl.BlockSpec((C, W_RET), lambda b, c: (0, 0))],
        out_specs=pl.BlockSpec((C, W_RET), tile),
        out_shape=jax.ShapeDtypeStruct((T, W_RET), BF16),
        scratch_shapes=[pltpu.VMEM((RET_HEADS, RET_DK, RET_DV), F32)],
        compiler_params=pltpu.CompilerParams(dimension_semantics=("parallel", "arbitrary")),
        name="retention",
    )(q, k, v, g, jnp.asarray(decay_in, F32), jnp.asarray(q_decay, F32), jnp.asarray(k_decay, F32))


def _out_proj_kernel(oa_ref, ob_ref, oc_ref, x_ref, ba_ref, bc_ref, wa_ref, wb_ref, wc_ref, g_ref, b_ref,
                     y_ref, *, alpha):
    na = _rms_norm(oa_ref[...], ba_ref[...]).astype(BF16)
    nc = _rms_norm(oc_ref[...], bc_ref[...]).astype(BF16)
    mix = _dot(na, wa_ref[...]) + _dot(ob_ref[...], wb_ref[...]) + _dot(nc, wc_ref[...])
    y_ref[...] = _layer_norm(alpha * x_ref[...] + mix, g_ref[...], b_ref[...])


def _out_proj(oa, ob, oc, x, beta_a, beta_c, wa, wb, wc, g, b, *, alpha, tm):
    T, D = x.shape
    row = lambda i: (i, 0)
    const = lambda i: (0, 0)
    full = lambda a: pl.BlockSpec(a.shape, const)
    return pl.pallas_call(
        functools.partial(_out_proj_kernel, alpha=alpha),
        grid=(T // tm,),
        in_specs=[pl.BlockSpec((tm, oa.shape[1]), row), pl.BlockSpec((tm, ob.shape[1]), row),
                  pl.BlockSpec((tm, oc.shape[1]), row), pl.BlockSpec((tm, D), row),
                  full(beta_a), full(beta_c), full(wa), full(wb), full(wc), full(g), full(b)],
        out_specs=pl.BlockSpec((tm, D), row),
        out_shape=jax.ShapeDtypeStruct((T, D), F32),
        compiler_params=pltpu.CompilerParams(dimension_semantics=("parallel",)),
        name="out_proj",
    )(oa, ob, oc, x, beta_a, beta_c, wa, wb, wc, g, b)


def _ffn_kernel(x_ref, wg_ref, wu_ref, wd_ref, g_ref, b_ref, y_ref, xb_ref, acc_ref, *, alpha):
    j = pl.program_id(1)

    @pl.when(j == 0)
    def _():
        xb_ref[...] = x_ref[...].astype(BF16)
        acc_ref[...] = jnp.zeros_like(acc_ref)

    xb = xb_ref[...]
    h = _silu(_dot(xb, wg_ref[...])) * _dot(xb, wu_ref[...])
    acc_ref[...] += _dot(h.astype(BF16), wd_ref[...])

    @pl.when(j == pl.num_programs(1) - 1)
    def _():
        y_ref[...] = _layer_norm(alpha * x_ref[...] + acc_ref[...], g_ref[...], b_ref[...])


def _ffn(x, wg, wu, wd, g, b, *, alpha, tm, tf):
    T, D = x.shape
    F = wg.shape[1]
    return pl.pallas_call(
        functools.partial(_ffn_kernel, alpha=alpha),
        grid=(T // tm, F // tf),
        in_specs=[pl.BlockSpec((tm, D), lambda i, j: (i, 0)),
                  pl.BlockSpec((D, tf), lambda i, j: (0, j)),
                  pl.BlockSpec((D, tf), lambda i, j: (0, j)),
                  pl.BlockSpec((tf, D), lambda i, j: (j, 0)),
                  pl.BlockSpec((1, D), lambda i, j: (0, 0)),
                  pl.BlockSpec((1, D), lambda i, j: (0, 0))],
        out_specs=pl.BlockSpec((tm, D), lambda i, j: (i, 0)),
        out_shape=jax.ShapeDtypeStruct((T, D), F32),
        scratch_shapes=[pltpu.VMEM((tm, D), BF16), pltpu.VMEM((tm, D), F32)],
        compiler_params=pltpu.CompilerParams(dimension_semantics=("parallel", "arbitrary")),
        name="ffn",
    )(x, wg, wu, wd, g, b)


def _router_kernel(x_ref, rhi_ref, rlo_ref, gates_ref):
    x = x_ref[...]
    x_hi = x.astype(BF16)
    x_lo = (x - x_hi.astype(F32)).astype(BF16)
    logits = _dot(x_hi, rhi_ref[...]) + _dot(x_lo, rhi_ref[...]) + _dot(x_hi, rlo_ref[...])
    lane = lax.broadcasted_iota(jnp.int32, logits.shape, 1)
    logits = jnp.where(lane < N_EXPERTS, logits, -jnp.inf)
    m1 = jnp.max(logits, -1, keepdims=True)
    i1 = jnp.min(jnp.where(logits == m1, lane, LANES), -1, keepdims=True)
    rest = jnp.where(lane == i1, -jnp.inf, logits)
    m2 = jnp.max(rest, -1, keepdims=True)
    i2 = jnp.min(jnp.where(rest == m2, lane, LANES), -1, keepdims=True)
    e2 = jnp.exp(m2 - m1)
    w1 = 1.0 / (1.0 + e2)
    w2 = e2 / (1.0 + e2)
    gates_ref[...] = jnp.where(lane == i1, w1, 0.0) + jnp.where(lane == i2, w2, 0.0)


def _router(x, r_hi, r_lo, *, tm):
    T, D = x.shape
    return pl.pallas_call(
        _router_kernel,
        grid=(T // tm,),
        in_specs=[pl.BlockSpec((tm, D), lambda i: (i, 0)),
                  pl.BlockSpec((D, LANES), lambda i: (0, 0)),
                  pl.BlockSpec((D, LANES), lambda i: (0, 0))],
        out_specs=pl.BlockSpec((tm, LANES), lambda i: (i, 0)),
        out_shape=jax.ShapeDtypeStruct((T, LANES), F32),
        compiler_params=pltpu.CompilerParams(dimension_semantics=("parallel",)),
        name="router",
    )(x, r_hi, r_lo)


def _moe_kernel(x_ref, gates_ref, wg_ref, wu_ref, wd_ref, g_ref, b_ref, y_ref, xb_ref, acc_ref, gate_ref, *, alpha):
    e = pl.program_id(1)
    j = pl.program_id(2)

    @pl.when((e == 0) & (j == 0))
    def _():
        xb_ref[...] = x_ref[...].astype(BF16)
        acc_ref[...] = jnp.zeros_like(acc_ref)

    @pl.when(j == 0)
    def _():
        gates = gates_ref[...]
        lane = lax.broadcasted_iota(jnp.int32, gates.shape, 1)
        gate_ref[...] = jnp.sum(jnp.where(lane == e, gates, 0.0), -1, keepdims=True)

    xb = xb_ref[...]
    h = _silu(_dot(xb, wg_ref[...])) * _dot(xb, wu_ref[...]) * gate_ref[...]
    acc_ref[...] += _dot(h.astype(BF16), wd_ref[...])

    @pl.when((e == pl.num_programs(1) - 1) & (j == pl.num_programs(2) - 1))
    def _():
        y_ref[...] = _layer_norm(alpha * x_ref[...] + acc_ref[...], g_ref[...], b_ref[...])


def _moe(x, gates, wg, wu, wd, g, b, *, alpha, tm, tf):
    T, D = x.shape
    E, _, F = wg.shape
    return pl.pallas_call(
        functools.partial(_moe_kernel, alpha=alpha),
        grid=(T // tm, E, F // tf),
        in_specs=[pl.BlockSpec((tm, D), lambda i, e, j: (i, 0)),
                  pl.BlockSpec((tm, LANES), lambda i, e, j: (i, 0)),
                  pl.BlockSpec((None, D, tf), lambda i, e, j: (e, 0, j)),
                  pl.BlockSpec((None, D, tf), lambda i, e, j: (e, 0, j)),
                  pl.BlockSpec((None, tf, D), lambda i, e, j: (e, j, 0)),
                  pl.BlockSpec((1, D), lambda i, e, j: (0, 0)),
                  pl.BlockSpec((1, D), lambda i, e, j: (0, 0))],
        out_specs=pl.BlockSpec((tm, D), lambda i, e, j: (i, 0)),
        out_shape=jax.ShapeDtypeStruct((T, D), F32),
        scratch_shapes=[pltpu.VMEM((tm, D), BF16), pltpu.VMEM((tm, D), F32), pltpu.VMEM((tm, 1), F32)],
        compiler_params=pltpu.CompilerParams(dimension_semantics=("parallel", "arbitrary", "arbitrary")),
        name="moe",
    )(x, gates, wg, wu, wd, g, b)


def _pad_cols(w, width):
    return jnp.pad(w, ((0, 0), (0, width - w.shape[1])))


def _swap_halves(w, head_dim):
    d = w.shape[1]
    idx = np.arange(d).reshape(d // head_dim, 2, head_dim // 2)[:, ::-1, :].reshape(d)
    return w[:, idx]


def _prep_in_weights(w_in, w_uq, w_ukv):
    sizes = (MLA_Q_RANK, MLA_KV_RANK, MLA_ROPE, W_RET, W_RET, W_RET, W_RET, W_MOBA, W_MOBA, W_MOBA)
    offs = np.cumsum((0,) + sizes)
    c_q, c_kv, k_rope, rq, rk, rv, rg, mq, mk, mv = (w_in[:, offs[n]:offs[n + 1]] for n in range(len(sizes)))
    w1 = jnp.concatenate([c_q, c_kv, _pad_cols(k_rope, LANES), _pad_cols(_swap_halves(k_rope, MLA_ROPE), LANES)], 1)
    w2 = jnp.concatenate([_pad_cols(w, RET_PAD) for w in
                          (rq, _swap_halves(rq, RET_DK), rk, _swap_halves(rk, RET_DK), rv, rg)], 1)
    def head_slots(w):
        w = w.reshape(w.shape[0], MOBA_HEADS, MOBA_DH)
        return jnp.pad(w, ((0, 0), (0, 0), (0, MOBA_SLOT - MOBA_DH))).reshape(w.shape[0], MOBA_SLOTS)

    w3 = jnp.concatenate([head_slots(mq), head_slots(mk), _pad_cols(mv, MOBA_PAD)], 1)

    dq = MLA_NOPE + MLA_ROPE
    uq = w_uq.reshape(MLA_Q_RANK, MLA_HEADS, dq)
    uq_rope_sw = _swap_halves(uq[:, :, MLA_NOPE:].reshape(MLA_Q_RANK, -1), MLA_ROPE).reshape(MLA_Q_RANK, MLA_HEADS, MLA_ROPE)
    zpad = jnp.zeros((MLA_Q_RANK, MLA_HEADS, MLA_SLOT - dq), w_uq.dtype)
    wuq = jnp.concatenate([uq, zpad], 2).reshape(MLA_Q_RANK, -1)
    wuq_sw = jnp.concatenate([jnp.zeros_like(uq[:, :, :MLA_NOPE]), uq_rope_sw, zpad], 2).reshape(MLA_Q_RANK, -1)
    ukv = w_ukv.reshape(MLA_KV_RANK, MLA_HEADS, MLA_NOPE + MLA_V)
    k_part = jnp.concatenate([ukv[:, :, :MLA_NOPE], jnp.zeros((MLA_KV_RANK, MLA_HEADS, MLA_SLOT - MLA_NOPE), w_ukv.dtype)], 2)
    wukv = jnp.concatenate([k_part.reshape(MLA_KV_RANK, -1), ukv[:, :, MLA_NOPE:].reshape(MLA_KV_RANK, -1)], 1)
    return tuple(w.astype(BF16) for w in (w1, w2, w3)), tuple(w.astype(BF16) for w in (wuq, wuq_sw, wukv))


def _rope_key_placement():
    p = np.zeros((LANES, MLA_HEADS * MLA_SLOT), np.float32)
    for h in range(MLA_HEADS):
        for j in range(MLA_ROPE):
            p[j, h * MLA_SLOT + MLA_NOPE + j] = 1.0
    return jnp.asarray(p, BF16)


def _rope_tables(seq):
    def cos_sin(dim):
        pos = jnp.arange(seq, dtype=F32)
        inv = ROPE_THETA ** (-jnp.arange(0, dim, 2, dtype=F32) / dim)
        ang = pos[:, None] * inv[None, :]
        c, s = jnp.cos(ang), jnp.sin(ang)
        return jnp.concatenate([c, c], 1), jnp.concatenate([-s, s], 1)

    ca, sa = cos_sin(MLA_ROPE)
    cb, sb = cos_sin(RET_DK)
    scale = (MLA_NOPE + MLA_ROPE) ** -0.5 * LOG2E
    ones = jnp.ones((seq, MLA_NOPE), F32)
    zeros_n = jnp.zeros((seq, MLA_NOPE), F32)
    zpad = jnp.zeros((seq, MLA_SLOT - MLA_NOPE - MLA_ROPE), F32)
    cq = jnp.tile(jnp.concatenate([ones, ca, zpad], 1), (1, MLA_HEADS)) * scale
    sq = jnp.tile(jnp.concatenate([zeros_n, sa, zpad], 1), (1, MLA_HEADS)) * scale
    ck = _pad_cols(ca, LANES)
    sk = _pad_cols(sa, LANES)
    cr = _pad_cols(jnp.tile(cb, (1, RET_HEADS)), RET_PAD)
    sr = _pad_cols(jnp.tile(sb, (1, RET_HEADS)), RET_PAD)
    return cq, sq, ck, sk, cr, sr


def kernel(x, ln_emb_g, ln_emb_b, w_in, q_norm_g, kv_norm_g, w_uq, w_ukv, beta_mla, beta_moba, w_o, ln1_g, ln1_b,
           ffn_w_gate, ffn_w_up, ffn_w_down, router, exp_w_gate, exp_w_up, exp_w_down, ln2_g, ln2_b):
    B, S, D = x.shape
    depth = w_in.shape[0]
    alpha = float((2 * depth) ** 0.25)
    assert S % MOBA_BLOCK == 0 and S % RET_CHUNK == 0
    tm = 512 if S % 512 == 0 else MOBA_BLOCK
    row = lambda v: v.reshape(1, -1).astype(F32)

    tables = _rope_tables(S)
    pk = _rope_key_placement()
    alibi_lanes = _moba_alibi_lanes(tm)
    xs = x.reshape(B * S, D)
    for l in range(depth):
        w123, wmla = _prep_in_weights(w_in[l], w_uq[l], w_ukv[l])
        weights = (*w123, row(q_norm_g[l]), row(kv_norm_g[l]), *wmla, pk, *alibi_lanes)
        ln = (row(ln_emb_g), row(ln_emb_b)) if l == 0 else None
        outs = _in_proj(xs, ln, weights, tables, seq=S, tm=tm)
        if l == 0:
            xs, *outs = outs
        qm, km, vm, rq, rk, rv, rg, mq, mk, mv, kmean = outs
        o_a = _mla_attention(qm, km, vm, batch=B, seq=S, tq=MOBA_BLOCK)
        o_b = _retention(rq, rk, rv, rg, batch=B, seq=S)
        o_c = _moba_attention(mq, mk, mv, kmean, batch=B, seq=S)
        wo = w_o[l].astype(BF16)
        xs = _out_proj(o_a, o_b, o_c, xs, row(beta_mla[l]), row(beta_moba[l]),
                       wo[:W_MLA_OUT], wo[W_MLA_OUT:W_MLA_OUT + W_RET], wo[W_MLA_OUT + W_RET:],
                       row(ln1_g[l]), row(ln1_b[l]), alpha=alpha, tm=tm)
        j = l // 2
        if l % 2 == 0:
            xs = _ffn(xs, ffn_w_gate[j].astype(BF16), ffn_w_up[j].astype(BF16), ffn_w_down[j].astype(BF16),
                      row(ln2_g[l]), row(ln2_b[l]), alpha=alpha, tm=tm, tf=256)
        else:
            r = router[j].astype(F32)
            r_hi = r.astype(BF16)
            r_lo = (r - r_hi.astype(F32)).astype(BF16)
            gates = _router(xs, _pad_cols(r_hi, LANES), _pad_cols(r_lo, LANES), tm=tm)
            xs = _moe(xs, gates, exp_w_gate[j].astype(BF16), exp_w_up[j].astype(BF16), exp_w_down[j].astype(BF16),
                      row(ln2_g[l]), row(ln2_b[l]), alpha=alpha, tm=tm, tf=512)
    return xs.reshape(B, S, D)
```

```python
import functools

import numpy as np
import jax
import jax.numpy as jnp
from jax import lax
from jax.experimental import pallas as pl
from jax.experimental.pallas import tpu as pltpu

F32 = jnp.float32
BF16 = jnp.bfloat16

MLA_HEADS, MLA_NOPE, MLA_ROPE, MLA_V = 6, 64, 32, 64
MLA_Q_RANK, MLA_KV_RANK = 256, 128
RET_HEADS, RET_DK, RET_DV, RET_CHUNK = 5, 64, 64, 128
MOBA_HEADS, MOBA_DH, MOBA_BLOCK, MOBA_TOPK = 5, 64, 256, 3
ROPE_THETA = 10000.0
LN_EPS = 1e-5
RMS_EPS = 1e-6
N_EXPERTS = 8
W_MLA_OUT = MLA_HEADS * MLA_V
W_RET = RET_HEADS * RET_DK
W_MOBA = MOBA_HEADS * MOBA_DH

LANES = 128
MLA_SLOT = 128
NEG = -1e30


def _pad_to(n, m):
    return -(-n // m) * m


RET_PAD = _pad_to(W_RET, LANES)
MOBA_PAD = _pad_to(W_MOBA, LANES)
MOBA_SLOT = 128
MOBA_SLOTS = MOBA_HEADS * MOBA_SLOT
LOG2E = 1.4426950408889634


def _layer_norm(x, g, b):
    mu = jnp.mean(x, -1, keepdims=True)
    xc = x - mu
    var = jnp.mean(xc * xc, -1, keepdims=True)
    return xc * lax.rsqrt(var + LN_EPS) * g + b


def _rms_norm(x, g):
    return x * lax.rsqrt(jnp.mean(x * x, -1, keepdims=True) + RMS_EPS) * g


def _silu(x):
    return x / (1.0 + jnp.exp(-x))


def _dot(a, b):
    return jnp.dot(a, b, preferred_element_type=F32)


def _dot_nt(a, b):
    return lax.dot_general(a, b, (((1,), (1,)), ((), ())), preferred_element_type=F32)


def _in_proj_kernel(*refs, apply_ln, blocks_per_tile):
    if apply_ln:
        x_ref, lng_ref, lnb_ref, *refs = refs
    else:
        x_ref, *refs = refs
    (w1_ref, w2_ref, w3_ref, qg_ref, kvg_ref, wuq_ref, wuqs_ref, wukv_ref, pk_ref, mqa_ref, mka_ref,
     cq_ref, sq_ref, ck_ref, sk_ref, cr_ref, sr_ref, *outs) = refs
    if apply_ln:
        x0_ref, *outs = outs
    (qm_ref, km_ref, vm_ref, rq_ref, rk_ref, rv_ref, rg_ref, mq_ref, mk_ref, mv_ref, kmean_ref) = outs

    x = x_ref[...]
    if apply_ln:
        x = _layer_norm(x, lng_ref[...], lnb_ref[...])
        x0_ref[...] = x
    xb = x.astype(BF16)

    ha = _dot(xb, w1_ref[...])
    qn = _rms_norm(ha[:, :MLA_Q_RANK], qg_ref[...]).astype(BF16)
    q = _dot(qn, wuq_ref[...]) * cq_ref[...] + _dot(qn, wuqs_ref[...]) * sq_ref[...]
    qm_ref[...] = q.astype(BF16)
    kvn = _rms_norm(ha[:, MLA_Q_RANK:MLA_Q_RANK + MLA_KV_RANK], kvg_ref[...]).astype(BF16)
    kv = _dot(kvn, wukv_ref[...])
    o = MLA_Q_RANK + MLA_KV_RANK
    kpe = ha[:, o:o + LANES] * ck_ref[...] + ha[:, o + LANES:o + 2 * LANES] * sk_ref[...]
    k_width = MLA_HEADS * MLA_SLOT
    km_ref[...] = (kv[:, :k_width] + _dot(kpe.astype(BF16), pk_ref[...])).astype(BF16)
    vm_ref[...] = kv[:, k_width:].T.astype(BF16)

    hb = _dot(xb, w2_ref[...])
    P = RET_PAD
    rq = hb[:, 0:P] * cr_ref[...] + hb[:, P:2 * P] * sr_ref[...]
    rk = (hb[:, 2 * P:3 * P] * cr_ref[...] + hb[:, 3 * P:4 * P] * sr_ref[...]) * (RET_DK ** -0.5)
    rq_ref[...] = rq[:, :W_RET].astype(BF16)
    rk_ref[...] = rk[:, :W_RET].astype(BF16)
    rv_ref[...] = hb[:, 4 * P:4 * P + W_RET].astype(BF16)
    rg_ref[...] = hb[:, 5 * P:5 * P + W_RET]

    hc = _dot(xb, w3_ref[...])
    P = MOBA_SLOTS
    mq_ref[...] = (hc[:, :P] * (MOBA_DH ** -0.5 * LOG2E) + mqa_ref[...]).astype(BF16)
    mk = hc[:, P:2 * P]
    mk_ref[...] = (mk + mka_ref[...]).astype(BF16)
    mv_ref[...] = hc[:, 2 * P:].T.astype(BF16)
    for bi in range(blocks_per_tile):
        kmean_ref[bi] = jnp.mean(mk[bi * MOBA_BLOCK:(bi + 1) * MOBA_BLOCK], axis=0, keepdims=True)


def _in_proj(x, ln, weights, tables, *, seq, tm):
    T, D = x.shape
    n_tiles = T // tm
    tiles_per_seq = seq // tm
    bpt = tm // MOBA_BLOCK
    apply_ln = ln is not None
    row = lambda i: (i, 0)
    const = lambda i: (0, 0)
    pos = lambda i: (i % tiles_per_seq, 0)

    in_specs, args = [pl.BlockSpec((tm, D), row)], [x]
    if apply_ln:
        in_specs += [pl.BlockSpec((1, D), const)] * 2
        args += list(ln)
    for w in weights:
        in_specs.append(pl.BlockSpec(w.shape, const))
        args.append(w)
    for t in tables:
        in_specs.append(pl.BlockSpec((tm, t.shape[1]), pos))
        args.append(t)

    def out(width, dtype):
        return jax.ShapeDtypeStruct((T, width), dtype), pl.BlockSpec((tm, width), row)

    outs = []
    if apply_ln:
        outs.append(out(D, F32))
    def out_t(height, dtype):
        return jax.ShapeDtypeStruct((height, T), dtype), pl.BlockSpec((height, tm), lambda i: (0, i))

    outs += [out(MLA_HEADS * MLA_SLOT, BF16), out(MLA_HEADS * MLA_SLOT, BF16), out_t(W_MLA_OUT, BF16),
             out(W_RET, BF16), out(W_RET, BF16), out(W_RET, BF16), out(W_RET, F32),
             out(MOBA_SLOTS, BF16), out(MOBA_SLOTS, BF16), out_t(MOBA_PAD, BF16)]
    outs.append((jax.ShapeDtypeStruct((T // MOBA_BLOCK, 1, MOBA_SLOTS), F32),
                 pl.BlockSpec((bpt, 1, MOBA_SLOTS), lambda i: (i, 0, 0))))
    return pl.pallas_call(
        functools.partial(_in_proj_kernel, apply_ln=apply_ln, blocks_per_tile=bpt),
        grid=(n_tiles,),
        in_specs=in_specs,
        out_specs=[o[1] for o in outs],
        out_shape=[o[0] for o in outs],
        compiler_params=pltpu.CompilerParams(dimension_semantics=("parallel",)),
        name="in_proj_ln" if apply_ln else "in_proj",
    )(*args)


SUM_ROWS = 16


def _softmax_step(s, vt, m, acc, query_bias=None):
    smax = jnp.max(s, 0, keepdims=True)
    if query_bias is not None:
        smax = smax + query_bias
    m_new = jnp.maximum(m, smax)
    shift = m_new if query_bias is None else m_new - query_bias
    p = jnp.exp2(s - shift).astype(BF16)
    vt_ones = jnp.concatenate([vt, jnp.ones((SUM_ROWS, vt.shape[1]), BF16)], axis=0)
    acc = jnp.exp2(m - m_new) * acc + _dot(vt_ones, p)
    return m_new, acc


def _softmax_init(n_heads, dv, tq):
    return tuple((jnp.full((1, tq), NEG, F32), jnp.zeros((dv + SUM_ROWS, tq), F32)) for _ in range(n_heads))


def _softmax_finish(carry, dv):
    return [acc[:dv] / acc[dv:dv + 1] for _, acc in carry]


def _all_heads(n_heads, score, update, carry):
    scores = [score(h) for h in range(n_heads)]
    return tuple(update(h, scores[h], carry[h]) for h in range(n_heads))


def _causal_mask(tq):
    keys = lax.broadcasted_iota(jnp.int32, (tq, tq), 0)
    queries = lax.broadcasted_iota(jnp.int32, (tq, tq), 1)
    return keys <= queries


def _mla_attn_kernel(q_ref, k_ref, vt_ref, o_ref, *, tq):
    i = pl.program_id(1)
    causal = _causal_mask(tq)
    slots = [slice(MLA_SLOT * h, MLA_SLOT * (h + 1)) for h in range(MLA_HEADS)]

    def block(j, carry, diagonal):
        start = pl.multiple_of(j * tq, tq)

        def score(h):
            return _dot_nt(k_ref[pl.ds(start, tq), slots[h]], q_ref[:, slots[h]])

        def update(h, s, state):
            s = jnp.where(causal, s, NEG) if diagonal else s
            return _softmax_step(s, vt_ref[MLA_V * h:MLA_V * (h + 1), pl.ds(start, tq)], *state)

        return _all_heads(MLA_HEADS, score, update, carry)

    carry = block(i, _softmax_init(MLA_HEADS, MLA_V, tq), True)
    carry = lax.fori_loop(0, i, functools.partial(block, diagonal=False), carry)
    out = jnp.concatenate(_softmax_finish(carry, MLA_V), axis=0)
    o_ref[...] = out.T


def _mla_attention(q, k, vt, *, batch, seq, tq):
    T = q.shape[0]
    nq = seq // tq
    return pl.pallas_call(
        functools.partial(_mla_attn_kernel, tq=tq),
        grid=(batch, nq),
        in_specs=[pl.BlockSpec((tq, q.shape[1]), lambda b, i: (b * nq + i, 0)),
                  pl.BlockSpec((seq, k.shape[1]), lambda b, i: (b, 0)),
                  pl.BlockSpec((vt.shape[0], seq), lambda b, i: (0, b))],
        out_specs=pl.BlockSpec((tq, W_MLA_OUT), lambda b, i: (b * nq + i, 0)),
        out_shape=jax.ShapeDtypeStruct((T, W_MLA_OUT), F32),
        compiler_params=pltpu.CompilerParams(dimension_semantics=("parallel", "arbitrary")),
        name="mla_attention",
    )(q, k, vt)


SLOPE_PARTS = 3


def _moba_slopes_log2():
    return tuple(float(np.sum(_slope_parts(h), dtype=np.float32)) for h in range(MOBA_HEADS))


def _slope_parts(h):
    rest = np.float32(2.0 ** (-8.0 * (h + 1.0) / MOBA_HEADS) * LOG2E)
    parts = []
    for _ in range(SLOPE_PARTS):
        piece = np.float32(rest.astype(BF16))
        parts.append(piece)
        rest = np.float32(rest - piece)
    return np.asarray(parts, np.float32)


def _moba_alibi_lanes(rows):
    q_add = np.zeros((1, MOBA_SLOTS), np.float32)
    k_add = np.zeros((rows, MOBA_SLOTS), np.float32)
    offset = (np.arange(rows) % MOBA_BLOCK).astype(np.float32)
    for h in range(MOBA_HEADS):
        lanes = slice(h * MOBA_SLOT + MOBA_DH, h * MOBA_SLOT + MOBA_DH + SLOPE_PARTS)
        q_add[0, lanes] = _slope_parts(h)
        k_add[:, lanes] = offset[:, None]
    return jnp.asarray(q_add), jnp.asarray(k_add)


def _moba_kernel(q_ref, k_ref, vt_ref, kmean_ref, o_ref, bias_ref, *, n_blocks, slopes):
    tq = MOBA_BLOCK
    i = pl.program_id(1)
    causal = _causal_mask(tq)
    blk_id = lax.broadcasted_iota(jnp.int32, (n_blocks, tq), 0)
    past = blk_id < i

    slots = [slice(MOBA_SLOT * h, MOBA_SLOT * (h + 1)) for h in range(MOBA_HEADS)]
    for h in range(MOBA_HEADS):
        q = q_ref[:, slots[h]]
        km = kmean_ref[:, 0, slots[h]]
        km_hi = km.astype(BF16)
        km_lo = (km - km_hi.astype(F32)).astype(BF16)
        gate = _dot_nt(km_hi, q) + _dot_nt(km_lo, q)
        gate = jnp.where(past, gate, -jnp.inf)
        rank = jnp.zeros((n_blocks, tq), F32)
        for mb in range(n_blocks):
            gm = gate[mb:mb + 1, :]
            beats = (gm > gate) | ((gm == gate) & (blk_id > mb))
            rank = rank + jnp.where(beats, 1.0, 0.0)
        chosen = past & (rank < MOBA_TOPK)
        bias = jnp.where(chosen, 0.0, NEG)
        for nb in range(n_blocks):
            bias_ref[h, nb] = bias[nb:nb + 1, :]

    def block(j, carry, diagonal):
        start = pl.multiple_of(j * tq, tq)

        def score(h):
            return _dot_nt(k_ref[pl.ds(start, tq), slots[h]], q_ref[:, slots[h]])

        def update(h, s, state):
            vt = vt_ref[MOBA_DH * h:MOBA_DH * (h + 1), pl.ds(start, tq)]
            if diagonal:
                return _softmax_step(jnp.where(causal, s, NEG), vt, *state)
            query_bias = bias_ref[h, j] + slopes[h] * ((j - i) * tq).astype(F32)
            return _softmax_step(s, vt, *state, query_bias=query_bias)

        return _all_heads(MOBA_HEADS, score, update, carry)

    carry = block(i, _softmax_init(MOBA_HEADS, MOBA_DH, tq), True)
    carry = lax.fori_loop(0, i, functools.partial(block, diagonal=False), carry)
    pad = jnp.zeros((MOBA_PAD - W_MOBA, tq), F32)
    out = jnp.concatenate(_softmax_finish(carry, MOBA_DH) + [pad], axis=0)
    o_ref[...] = out.T[:, :W_MOBA]


def _moba_attention(q, k, vt, kmean, *, batch, seq):
    T = q.shape[0]
    tq = MOBA_BLOCK
    nq = seq // tq
    slopes = _moba_slopes_log2()
    return pl.pallas_call(
        functools.partial(_moba_kernel, n_blocks=nq, slopes=slopes),
        grid=(batch, nq),
        in_specs=[pl.BlockSpec((tq, MOBA_SLOTS), lambda b, i: (b * nq + i, 0)),
                  pl.BlockSpec((seq, MOBA_SLOTS), lambda b, i: (b, 0)),
                  pl.BlockSpec((vt.shape[0], seq), lambda b, i: (0, b)),
                  pl.BlockSpec((nq, 1, MOBA_SLOTS), lambda b, i: (b, 0, 0))],
        out_specs=pl.BlockSpec((tq, W_MOBA), lambda b, i: (b * nq + i, 0)),
        out_shape=jax.ShapeDtypeStruct((T, W_MOBA), F32),
        scratch_shapes=[pltpu.VMEM((MOBA_HEADS, nq, 1, tq), F32)],
        compiler_params=pltpu.CompilerParams(dimension_semantics=("parallel", "arbitrary")),
        name="moba_attention",
    )(q, k, vt, kmean)


def _retention_kernel(q_ref, k_ref, v_ref, g_ref, din_ref, qdec_ref, kdec_ref, o_ref, state_ref, *, chunk_decay):
    @pl.when(pl.program_id(1) == 0)
    def _():
        state_ref[...] = jnp.zeros_like(state_ref)

    for h in range(RET_HEADS):
        hs = slice(RET_DK * h, RET_DK * (h + 1))
        q = q_ref[:, hs]
        k = k_ref[:, hs]
        v = v_ref[:, hs]
        intra = _dot_nt(q, k) * din_ref[h]
        o = _dot(intra.astype(BF16), v)
        state = state_ref[h]
        o = o + _dot(q, state.astype(BF16)) * qdec_ref[:, hs]
        kd = (k.astype(F32) * kdec_ref[:, hs]).astype(BF16)
        kv = lax.dot_general(kd, v, (((0,), (0,)), ((), ())), preferred_element_type=F32)
        state_ref[h] = state * chunk_decay[h] + kv
        mu = jnp.mean(o, -1, keepdims=True)
        oc = o - mu
        var = jnp.mean(oc * oc, -1, keepdims=True)
        g = g_ref[:, hs]
        o_ref[:, hs] = (_silu(g) * (oc * lax.rsqrt(var + RMS_EPS))).astype(BF16)


def _retention(q, k, v, g, *, batch, seq):
    T = q.shape[0]
    C = RET_CHUNK
    nc = seq // C
    log_gamma = np.log(1.0 - 2.0 ** (-5.0 - np.arange(RET_HEADS, dtype=np.float64)))
    idx = np.arange(C, dtype=np.float64)
    diff = idx[:, None] - idx[None, :]
    decay_in = np.where(diff >= 0, np.exp(log_gamma[:, None, None] * np.maximum(diff, 0.0)), 0.0)
    q_decay = np.repeat(np.exp(log_gamma[None, :] * (idx[:, None] + 1.0)), RET_DK, axis=1)
    k_decay = np.repeat(np.exp(log_gamma[None, :] * (C - 1.0 - idx[:, None])), RET_DK, axis=1)
    chunk_decay = tuple(float(c) for c in np.exp(log_gamma * C))
    tile = lambda b, c: (b * nc + c, 0)
    return pl.pallas_call(
        functools.partial(_retention_kernel, chunk_decay=chunk_decay),
        grid=(batch, nc),
        in_specs=[pl.BlockSpec((C, W_RET), tile)] * 4
        + [pl.BlockSpec((RET_HEADS, C, C), lambda b, c: (0, 0, 0)),
           pl.BlockSpec((C, W_RET), lambda b, c: (0, 0)),
           pl.BlockSpec((C, W_RET), lambda b, c: (0, 0))],
        out_specs=pl.BlockSpec((C, W_RET), tile),
        out_shape=jax.ShapeDtypeStruct((T, W_RET), BF16),
        scratch_shapes=[pltpu.VMEM((RET_HEADS, RET_DK, RET_DV), F32)],
        compiler_params=pltpu.CompilerParams(dimension_semantics=("parallel", "arbitrary")),
        name="retention",
    )(q, k, v, g, jnp.asarray(decay_in, F32), jnp.asarray(q_decay, F32), jnp.asarray(k_decay, F32))


def _out_proj_kernel(oa_ref, ob_ref, oc_ref, x_ref, ba_ref, bc_ref, wa_ref, wb_ref, wc_ref, g_ref, b_ref,
                     y_ref, *, alpha):
    na = _rms_norm(oa_ref[...], ba_ref[...]).astype(BF16)
    nc = _rms_norm(oc_ref[...], bc_ref[...]).astype(BF16)
    mix = _dot(na, wa_ref[...]) + _dot(ob_ref[...], wb_ref[...]) + _dot(nc, wc_ref[...])
    y_ref[...] = _layer_norm(alpha * x_ref[...] + mix, g_ref[...], b_ref[...])


def _out_proj(oa, ob, oc, x, beta_a, beta_c, wa, wb, wc, g, b, *, alpha, tm):
    T, D = x.shape
    row = lambda i: (i, 0)
    const = lambda i: (0, 0)
    full = lambda a: pl.BlockSpec(a.shape, const)
    return pl.pallas_call(
        functools.partial(_out_proj_kernel, alpha=alpha),
        grid=(T // tm,),
        in_specs=[pl.BlockSpec((tm, oa.shape[1]), row), pl.BlockSpec((tm, ob.shape[1]), row),
                  pl.BlockSpec((tm, oc.shape[1]), row), pl.BlockSpec((tm, D), row),
                  full(beta_a), full(beta_c), full(wa), full(wb), full(wc), full(g), full(b)],
        out_specs=pl.BlockSpec((tm, D), row),
        out_shape=jax.ShapeDtypeStruct((T, D), F32),
        compiler_params=pltpu.CompilerParams(dimension_semantics=("parallel",)),
        name="out_proj",
    )(oa, ob, oc, x, beta_a, beta_c, wa, wb, wc, g, b)


def _ffn_kernel(x_ref, wg_ref, wu_ref, wd_ref, g_ref, b_ref, y_ref, xb_ref, acc_ref, *, alpha):
    j = pl.program_id(1)

    @pl.when(j == 0)
    def _():
        xb_ref[...] = x_ref[...].astype(BF16)
        acc_ref[...] = jnp.zeros_like(acc_ref)

    xb = xb_ref[...]
    h = _silu(_dot(xb, wg_ref[...])) * _dot(xb, wu_ref[...])
    acc_ref[...] += _dot(h.astype(BF16), wd_ref[...])

    @pl.when(j == pl.num_programs(1) - 1)
    def _():
        y_ref[...] = _layer_norm(alpha * x_ref[...] + acc_ref[...], g_ref[...], b_ref[...])


def _ffn(x, wg, wu, wd, g, b, *, alpha, tm, tf):
    T, D = x.shape
    F = wg.shape[1]
    return pl.pallas_call(
        functools.partial(_ffn_kernel, alpha=alpha),
        grid=(T // tm, F // tf),
        in_specs=[pl.BlockSpec((tm, D), lambda i, j: (i, 0)),
                  pl.BlockSpec((D, tf), lambda i, j: (0, j)),
                  pl.BlockSpec((D, tf), lambda i, j: (0, j)),
                  pl.BlockSpec((tf, D), lambda i, j: (j, 0)),
                  pl.BlockSpec((1, D), lambda i, j: (0, 0)),
                  pl.BlockSpec((1, D), lambda i, j: (0, 0))],
        out_specs=pl.BlockSpec((tm, D), lambda i, j: (i, 0)),
        out_shape=jax.ShapeDtypeStruct((T, D), F32),
        scratch_shapes=[pltpu.VMEM((tm, D), BF16), pltpu.VMEM((tm, D), F32)],
        compiler_params=pltpu.CompilerParams(dimension_semantics=("parallel", "arbitrary")),
        name="ffn",
    )(x, wg, wu, wd, g, b)


SUBLANES = 8
R_E1, R_E2, R_W1, R_W2 = 0, 1, 2, 3


def _lane_pick(x, lane, idx):
    return jnp.sum(jnp.where(lane == idx, x, 0.0), -1, keepdims=True)


def _router_kernel(x_ref, rhi_ref, rlo_ref, route_ref, sel_ref):
    x = x_ref[...]
    x_hi = x.astype(BF16)
    x_lo = (x - x_hi.astype(F32)).astype(BF16)
    logits = _dot(x_hi, rhi_ref[...]) + _dot(x_lo, rhi_ref[...]) + _dot(x_hi, rlo_ref[...])
    lane = lax.broadcasted_iota(jnp.int32, logits.shape, 1)
    logits = jnp.where(lane < N_EXPERTS, logits, -jnp.inf)
    m1 = jnp.max(logits, -1, keepdims=True)
    i1 = jnp.min(jnp.where(logits == m1, lane, LANES), -1, keepdims=True)
    rest = jnp.where(lane == i1, -jnp.inf, logits)
    m2 = jnp.max(rest, -1, keepdims=True)
    i2 = jnp.min(jnp.where(rest == m2, lane, LANES), -1, keepdims=True)
    e2 = jnp.exp(m2 - m1)
    w1 = 1.0 / (1.0 + e2)
    w2 = e2 / (1.0 + e2)
    route = jnp.where(lane == R_E1, i1.astype(F32), 0.0) + jnp.where(lane == R_E2, i2.astype(F32), 0.0)
    route_ref[...] = route + jnp.where(lane == R_W1, w1, 0.0) + jnp.where(lane == R_W2, w2, 0.0)
    sel_ref[...] = jnp.where((lane == i1) | (lane == i2), 1.0, 0.0).astype(BF16)


def _router(x, r_hi, r_lo, *, tm):
    T, D = x.shape
    return pl.pallas_call(
        _router_kernel,
        grid=(T // tm,),
        in_specs=[pl.BlockSpec((tm, D), lambda i: (i, 0)),
                  pl.BlockSpec((D, LANES), lambda i: (0, 0)),
                  pl.BlockSpec((D, LANES), lambda i: (0, 0))],
        out_specs=[pl.BlockSpec((tm, LANES), lambda i: (i, 0)), pl.BlockSpec((tm, LANES), lambda i: (i, 0))],
        out_shape=[jax.ShapeDtypeStruct((T, LANES), F32), jax.ShapeDtypeStruct((T, LANES), BF16)],
        compiler_params=pltpu.CompilerParams(dimension_semantics=("parallel",)),
        name="router",
    )(x, r_hi, r_lo)


def _rank_kernel(sel_ref, rank_ref, count_ref, carry_ref):
    @pl.when(pl.program_id(0) == 0)
    def _():
        carry_ref[...] = jnp.zeros_like(carry_ref)

    sel = sel_ref[...]
    tm = sel.shape[0]
    earlier = lax.broadcasted_iota(jnp.int32, (tm, tm), 1) < lax.broadcasted_iota(jnp.int32, (tm, tm), 0)
    rank_ref[...] = _dot(jnp.where(earlier, 1.0, 0.0).astype(BF16), sel) + carry_ref[...]
    carry_ref[...] += jnp.sum(sel.astype(F32), 0, keepdims=True)
    count_ref[...] = carry_ref[...]


def _rank(sel, *, tm):
    T = sel.shape[0]
    return pl.pallas_call(
        _rank_kernel,
        grid=(T // tm,),
        in_specs=[pl.BlockSpec((tm, LANES), lambda i: (i, 0))],
        out_specs=[pl.BlockSpec((tm, LANES), lambda i: (i, 0)), pl.BlockSpec((1, LANES), lambda i: (0, 0))],
        out_shape=[jax.ShapeDtypeStruct((T, LANES), F32), jax.ShapeDtypeStruct((1, LANES), F32)],
        scratch_shapes=[pltpu.VMEM((1, LANES), F32)],
        compiler_params=pltpu.CompilerParams(dimension_semantics=("arbitrary",)),
        name="moe_rank",
    )(sel)


def _position_kernel(route_ref, rank_ref, off_ref, pos_ref):
    route = route_ref[...]
    lane = lax.broadcasted_iota(jnp.int32, route.shape, 1)
    e1 = _lane_pick(route, lane, R_E1).astype(jnp.int32)
    e2 = _lane_pick(route, lane, R_E2).astype(jnp.int32)
    slot = rank_ref[...] + off_ref[...]
    p1 = _lane_pick(slot, lane, e1)
    p2 = _lane_pick(slot, lane, e2)
    pos_ref[...] = (jnp.where(lane == 0, p1, 0.0) + jnp.where(lane == 1, p2, 0.0)).astype(jnp.int32)


def _positions(route, rank, offsets, *, tm):
    T = route.shape[0]
    return pl.pallas_call(
        _position_kernel,
        grid=(T // tm,),
        in_specs=[pl.BlockSpec((tm, LANES), lambda i: (i, 0)), pl.BlockSpec((tm, LANES), lambda i: (i, 0)),
                  pl.BlockSpec((1, LANES), lambda i: (0, 0))],
        out_specs=pl.BlockSpec((tm, LANES), lambda i: (i, 0)),
        out_shape=jax.ShapeDtypeStruct((T, LANES), jnp.int32),
        compiler_params=pltpu.CompilerParams(dimension_semantics=("parallel",)),
        name="moe_positions",
    )(route, rank, offsets)


def _row_copy(src_ref, src_row, dst_ref, dst_row, sem):
    return pltpu.make_async_copy(
        src_ref.at[pl.ds(pl.multiple_of(src_row * SUBLANES, SUBLANES), SUBLANES), :],
        dst_ref.at[pl.ds(pl.multiple_of(dst_row * SUBLANES, SUBLANES), SUBLANES), :],
        sem)


def _start_then_wait(n_rows, copies):
    def start(r, c):
        for cp in copies(r):
            cp.start()
        return c

    def wait(r, c):
        for cp in copies(r):
            cp.wait()
        return c

    lax.fori_loop(0, n_rows, start, 0)
    return lambda: lax.fori_loop(0, n_rows, wait, 0)


def _dispatch_kernel(p1_ref, p2_ref, x_ref, buf_in_ref, buf_ref, sem):
    del buf_in_ref
    wait_all = _start_then_wait(p1_ref.shape[-1], lambda r: (
        _row_copy(x_ref, r, buf_ref, p1_ref[0, r], sem), _row_copy(x_ref, r, buf_ref, p2_ref[0, r], sem)))
    wait_all()


def _dispatch(x_tiles, p1, p2, n_rows, *, tm):
    T = x_tiles.shape[0] // SUBLANES
    buf = jnp.zeros((n_rows * SUBLANES, LANES), F32)
    smem = lambda: pl.BlockSpec((None, 1, tm), lambda i: (i, 0, 0), memory_space=pltpu.SMEM)
    return pl.pallas_call(
        _dispatch_kernel,
        grid=(T // tm,),
        in_specs=[smem(), smem(), pl.BlockSpec((tm * SUBLANES, LANES), lambda i: (i, 0)),
                  pl.BlockSpec(memory_space=pl.ANY)],
        out_specs=pl.BlockSpec(memory_space=pl.ANY),
        out_shape=jax.ShapeDtypeStruct(buf.shape, F32),
        input_output_aliases={3: 0},
        scratch_shapes=[pltpu.SemaphoreType.DMA(())],
        compiler_params=pltpu.CompilerParams(dimension_semantics=("arbitrary",)),
        name="moe_dispatch",
    )(p1, p2, x_tiles, buf)


def _rows_from_tiles(ref, c, rows):
    return ref[pl.ds(c, rows, stride=SUBLANES), :]


def _expert_kernel(tile_expert_ref, n_used_ref, x_ref, wg_ref, wu_ref, wd_ref, y_ref, xb_ref, acc_ref):
    del tile_expert_ref
    g = pl.program_id(0)
    j = pl.program_id(1)
    tm = xb_ref.shape[0]
    used = g < n_used_ref[0]

    @pl.when(used & (j == 0))
    def _():
        for c in range(SUBLANES):
            xb_ref[:, c * LANES:(c + 1) * LANES] = _rows_from_tiles(x_ref, c, tm).astype(BF16)
        acc_ref[...] = jnp.zeros_like(acc_ref)

    @pl.when(used)
    def _():
        xb = xb_ref[...]
        h = _silu(_dot(xb, wg_ref[...])) * _dot(xb, wu_ref[...])
        acc_ref[...] += _dot(h.astype(BF16), wd_ref[...])

    @pl.when(j == pl.num_programs(1) - 1)
    def _():
        for c in range(SUBLANES):
            cols = acc_ref[:, c * LANES:(c + 1) * LANES]
            y_ref[pl.ds(c, tm, stride=SUBLANES), :] = jnp.where(used, cols, 0.0)


def _experts(buf, tile_expert, n_used, wg, wu, wd, *, tm, tf):
    n_tiles = buf.shape[0] // (tm * SUBLANES)
    E, D, F = wg.shape
    x_map = lambda g, j, te, nu: (g, 0)
    grid_spec = pltpu.PrefetchScalarGridSpec(
        num_scalar_prefetch=2,
        grid=(n_tiles, F // tf),
        in_specs=[pl.BlockSpec((tm * SUBLANES, LANES), x_map),
                  pl.BlockSpec((None, D, tf), lambda g, j, te, nu: (te[g], 0, j)),
                  pl.BlockSpec((None, D, tf), lambda g, j, te, nu: (te[g], 0, j)),
                  pl.BlockSpec((None, tf, D), lambda g, j, te, nu: (te[g], j, 0))],
        out_specs=pl.BlockSpec((tm * SUBLANES, LANES), x_map),
        scratch_shapes=[pltpu.VMEM((tm, D), BF16), pltpu.VMEM((tm, D), F32)],
    )
    return pl.pallas_call(
        _expert_kernel,
        grid_spec=grid_spec,
        out_shape=jax.ShapeDtypeStruct(buf.shape, F32),
        compiler_params=pltpu.CompilerParams(dimension_semantics=("parallel", "arbitrary")),
        name="moe_experts",
    )(tile_expert, n_used, buf, wg, wu, wd)


def _combine_kernel(p1_ref, p2_ref, x_ref, route_ref, y_hbm_ref, g_ref, b_ref, out_ref, y1_ref, y2_ref, sem, *, alpha):
    tm = x_ref.shape[0]
    wait_all = _start_then_wait(tm, lambda r: (
        _row_copy(y_hbm_ref, p1_ref[0, r], y1_ref, r, sem), _row_copy(y_hbm_ref, p2_ref[0, r], y2_ref, r, sem)))
    route = route_ref[...]
    lane = lax.broadcasted_iota(jnp.int32, route.shape, 1)
    w1 = _lane_pick(route, lane, R_W1)
    w2 = _lane_pick(route, lane, R_W2)
    wait_all()
    f = jnp.concatenate([w1 * _rows_from_tiles(y1_ref, c, tm) + w2 * _rows_from_tiles(y2_ref, c, tm)
                         for c in range(SUBLANES)], axis=1)
    out_ref[...] = _layer_norm(alpha * x_ref[...] + f, g_ref[...], b_ref[...])


def _combine(x, route, y_tiles, p1, p2, g, b, *, alpha, tm):
    T, D = x.shape
    smem = lambda: pl.BlockSpec((None, 1, tm), lambda i: (i, 0, 0), memory_space=pltpu.SMEM)
    return pl.pallas_call(
        functools.partial(_combine_kernel, alpha=alpha),
        grid=(T // tm,),
        in_specs=[smem(), smem(), pl.BlockSpec((tm, D), lambda i: (i, 0)), pl.BlockSpec((tm, LANES), lambda i: (i, 0)),
                  pl.BlockSpec(memory_space=pl.ANY),
                  pl.BlockSpec((1, D), lambda i: (0, 0)), pl.BlockSpec((1, D), lambda i: (0, 0))],
        out_specs=pl.BlockSpec((tm, D), lambda i: (i, 0)),
        out_shape=jax.ShapeDtypeStruct((T, D), F32),
        scratch_shapes=[pltpu.VMEM((tm * SUBLANES, LANES), F32), pltpu.VMEM((tm * SUBLANES, LANES), F32),
                        pltpu.SemaphoreType.DMA(())],
        compiler_params=pltpu.CompilerParams(dimension_semantics=("arbitrary",)),
        name="moe_combine",
    )(p1, p2, x, route, y_tiles, g, b)


def _tile_plan(counts, n_tiles, tm):
    padded = ((counts + tm - 1) // tm) * tm
    ends = jnp.cumsum(padded)
    offsets = ends - padded
    tile_start = jnp.arange(n_tiles, dtype=jnp.int32) * tm
    tile_expert = jnp.minimum(jnp.sum(tile_start[:, None] >= ends[None, :], axis=1), N_EXPERTS - 1).astype(jnp.int32)
    n_used = (ends[-1] // tm).astype(jnp.int32).reshape(1)
    tile_expert = jnp.where(jnp.arange(n_tiles) < n_used[0], tile_expert, tile_expert[jnp.maximum(n_used[0] - 1, 0)])
    return offsets, tile_expert, n_used


def _moe(x, router_w, wg, wu, wd, g, b, *, alpha, tm, tf):
    T, D = x.shape
    assert D == SUBLANES * LANES
    r = router_w.astype(F32)
    r_hi = r.astype(BF16)
    r_lo = (r - r_hi.astype(F32)).astype(BF16)
    route, sel = _router(x, _pad_cols(r_hi, LANES), _pad_cols(r_lo, LANES), tm=tm)
    rank, counts = _rank(sel, tm=tm)
    n_tiles = (2 * T) // tm + N_EXPERTS
    offsets, tile_expert, n_used = _tile_plan(counts[0, :N_EXPERTS].astype(jnp.int32), n_tiles, tm)
    pos = _positions(route, rank, _pad_cols(offsets.astype(F32).reshape(1, -1), LANES), tm=tm)
    p1 = pos[:, 0].reshape(T // tm, 1, tm)
    p2 = pos[:, 1].reshape(T // tm, 1, tm)
    buf = _dispatch(x.reshape(T * SUBLANES, LANES), p1, p2, n_tiles * tm, tm=tm)
    y_tiles = _experts(buf, tile_expert, n_used, wg, wu, wd, tm=tm, tf=tf)
    return _combine(x, route, y_tiles, p1, p2, g, b, alpha=alpha, tm=tm)


def _pad_cols(w, width):
    return jnp.pad(w, ((0, 0), (0, width - w.shape[1])))


def _swap_halves(w, head_dim):
    d = w.shape[1]
    idx = np.arange(d).reshape(d // head_dim, 2, head_dim // 2)[:, ::-1, :].reshape(d)
    return w[:, idx]


def _prep_in_weights(w_in, w_uq, w_ukv):
    sizes = (MLA_Q_RANK, MLA_KV_RANK, MLA_ROPE, W_RET, W_RET, W_RET, W_RET, W_MOBA, W_MOBA, W_MOBA)
    offs = np.cumsum((0,) + sizes)
    c_q, c_kv, k_rope, rq, rk, rv, rg, mq, mk, mv = (w_in[:, offs[n]:offs[n + 1]] for n in range(len(sizes)))
    w1 = jnp.concatenate([c_q, c_kv, _pad_cols(k_rope, LANES), _pad_cols(_swap_halves(k_rope, MLA_ROPE), LANES)], 1)
    w2 = jnp.concatenate([_pad_cols(w, RET_PAD) for w in
                          (rq, _swap_halves(rq, RET_DK), rk, _swap_halves(rk, RET_DK), rv, rg)], 1)
    def head_slots(w):
        w = w.reshape(w.shape[0], MOBA_HEADS, MOBA_DH)
        return jnp.pad(w, ((0, 0), (0, 0), (0, MOBA_SLOT - MOBA_DH))).reshape(w.shape[0], MOBA_SLOTS)

    w3 = jnp.concatenate([head_slots(mq), head_slots(mk), _pad_cols(mv, MOBA_PAD)], 1)

    dq = MLA_NOPE + MLA_ROPE
    uq = w_uq.reshape(MLA_Q_RANK, MLA_HEADS, dq)
    uq_rope_sw = _swap_halves(uq[:, :, MLA_NOPE:].reshape(MLA_Q_RANK, -1), MLA_ROPE).reshape(MLA_Q_RANK, MLA_HEADS, MLA_ROPE)
    zpad = jnp.zeros((MLA_Q_RANK, MLA_HEADS, MLA_SLOT - dq), w_uq.dtype)
    wuq = jnp.concatenate([uq, zpad], 2).reshape(MLA_Q_RANK, -1)
    wuq_sw = jnp.concatenate([jnp.zeros_like(uq[:, :, :MLA_NOPE]), uq_rope_sw, zpad], 2).reshape(MLA_Q_RANK, -1)
    ukv = w_ukv.reshape(MLA_KV_RANK, MLA_HEADS, MLA_NOPE + MLA_V)
    k_part = jnp.concatenate([ukv[:, :, :MLA_NOPE], jnp.zeros((MLA_KV_RANK, MLA_HEADS, MLA_SLOT - MLA_NOPE), w_ukv.dtype)], 2)
    wukv = jnp.concatenate([k_part.reshape(MLA_KV_RANK, -1), ukv[:, :, MLA_NOPE:].reshape(MLA_KV_RANK, -1)], 1)
    return tuple(w.astype(BF16) for w in (w1, w2, w3)), tuple(w.astype(BF16) for w in (wuq, wuq_sw, wukv))


def _rope_key_placement():
    p = np.zeros((LANES, MLA_HEADS * MLA_SLOT), np.float32)
    for h in range(MLA_HEADS):
        for j in range(MLA_ROPE):
            p[j, h * MLA_SLOT + MLA_NOPE + j] = 1.0
    return jnp.asarray(p, BF16)


def _rope_tables(seq):
    def cos_sin(dim):
        pos = jnp.arange(seq, dtype=F32)
        inv = ROPE_THETA ** (-jnp.arange(0, dim, 2, dtype=F32) / dim)
        ang = pos[:, None] * inv[None, :]
        c, s = jnp.cos(ang), jnp.sin(ang)
        return jnp.concatenate([c, c], 1), jnp.concatenate([-s, s], 1)

    ca, sa = cos_sin(MLA_ROPE)
    cb, sb = cos_sin(RET_DK)
    scale = (MLA_NOPE + MLA_ROPE) ** -0.5 * LOG2E
    ones = jnp.ones((seq, MLA_NOPE), F32)
    zeros_n = jnp.zeros((seq, MLA_NOPE), F32)
    zpad = jnp.zeros((seq, MLA_SLOT - MLA_NOPE - MLA_ROPE), F32)
    cq = jnp.tile(jnp.concatenate([ones, ca, zpad], 1), (1, MLA_HEADS)) * scale
    sq = jnp.tile(jnp.concatenate([zeros_n, sa, zpad], 1), (1, MLA_HEADS)) * scale
    ck = _pad_cols(ca, LANES)
    sk = _pad_cols(sa, LANES)
    cr = _pad_cols(jnp.tile(cb, (1, RET_HEADS)), RET_PAD)
    sr = _pad_cols(jnp.tile(sb, (1, RET_HEADS)), RET_PAD)
    return cq, sq, ck, sk, cr, sr


def kernel(x, ln_emb_g, ln_emb_b, w_in, q_norm_g, kv_norm_g, w_uq, w_ukv, beta_mla, beta_moba, w_o, ln1_g, ln1_b,
           ffn_w_gate, ffn_w_up, ffn_w_down, router, exp_w_gate, exp_w_up, exp_w_down, ln2_g, ln2_b):
    B, S, D = x.shape
    depth = w_in.shape[0]
    alpha = float((2 * depth) ** 0.25)
    assert S % MOBA_BLOCK == 0 and S % RET_CHUNK == 0
    tm = 512 if S % 512 == 0 else MOBA_BLOCK
    row = lambda v: v.reshape(1, -1).astype(F32)

    tables = _rope_tables(S)
    pk = _rope_key_placement()
    alibi_lanes = _moba_alibi_lanes(tm)
    xs = x.reshape(B * S, D)
    for l in range(depth):
        w123, wmla = _prep_in_weights(w_in[l], w_uq[l], w_ukv[l])
        weights = (*w123, row(q_norm_g[l]), row(kv_norm_g[l]), *wmla, pk, *alibi_lanes)
        ln = (row(ln_emb_g), row(ln_emb_b)) if l == 0 else None
        outs = _in_proj(xs, ln, weights, tables, seq=S, tm=tm)
        if l == 0:
            xs, *outs = outs
        qm, km, vm, rq, rk, rv, rg, mq, mk, mv, kmean = outs
        o_a = _mla_attention(qm, km, vm, batch=B, seq=S, tq=MOBA_BLOCK)
        o_b = _retention(rq, rk, rv, rg, batch=B, seq=S)
        o_c = _moba_attention(mq, mk, mv, kmean, batch=B, seq=S)
        wo = w_o[l].astype(BF16)
        xs = _out_proj(o_a, o_b, o_c, xs, row(beta_mla[l]), row(beta_moba[l]),
                       wo[:W_MLA_OUT], wo[W_MLA_OUT:W_MLA_OUT + W_RET], wo[W_MLA_OUT + W_RET:],
                       row(ln1_g[l]), row(ln1_b[l]), alpha=alpha, tm=tm)
        j = l // 2
        if l % 2 == 0:
            xs = _ffn(xs, ffn_w_gate[j].astype(BF16), ffn_w_up[j].astype(BF16), ffn_w_down[j].astype(BF16),
                      row(ln2_g[l]), row(ln2_b[l]), alpha=alpha, tm=tm, tf=256)
        else:
            xs = _moe(xs, router[j], exp_w_gate[j].astype(BF16), exp_w_up[j].astype(BF16), exp_w_down[j].astype(BF16),
                      row(ln2_g[l]), row(ln2_b[l]), alpha=alpha, tm=tm, tf=512)
    return xs.reshape(B, S, D)
```

```python
import functools

import numpy as np
import jax
import jax.numpy as jnp
from jax import lax
from jax.experimental import pallas as pl
from jax.experimental.pallas import tpu as pltpu

F32 = jnp.float32
BF16 = jnp.bfloat16

MLA_HEADS, MLA_NOPE, MLA_ROPE, MLA_V = 6, 64, 32, 64
MLA_Q_RANK, MLA_KV_RANK = 256, 128
RET_HEADS, RET_DK, RET_DV, RET_CHUNK = 5, 64, 64, 128
MOBA_HEADS, MOBA_DH, MOBA_BLOCK, MOBA_TOPK = 5, 64, 256, 3
ROPE_THETA = 10000.0
LN_EPS = 1e-5
RMS_EPS = 1e-6
N_EXPERTS = 8
W_MLA_OUT = MLA_HEADS * MLA_V
W_RET = RET_HEADS * RET_DK
W_MOBA = MOBA_HEADS * MOBA_DH

LANES = 128
MLA_SLOT = 128
NEG = -1e30


def _pad_to(n, m):
    return -(-n // m) * m


RET_PAD = _pad_to(W_RET, LANES)
MOBA_PAD = _pad_to(W_MOBA, LANES)
MOBA_SLOT = 128
MOBA_SLOTS = MOBA_HEADS * MOBA_SLOT
LOG2E = 1.4426950408889634


def _layer_norm(x, g, b):
    mu = jnp.mean(x, -1, keepdims=True)
    xc = x - mu
    var = jnp.mean(xc * xc, -1, keepdims=True)
    return xc * lax.rsqrt(var + LN_EPS) * g + b


def _rms_norm(x, g):
    return x * lax.rsqrt(jnp.mean(x * x, -1, keepdims=True) + RMS_EPS) * g


def _silu(x):
    return x / (1.0 + jnp.exp(-x))


def _dot(a, b):
    return jnp.dot(a, b, preferred_element_type=F32)


def _dot_nt(a, b):
    return lax.dot_general(a, b, (((1,), (1,)), ((), ())), preferred_element_type=F32)


def _in_proj_kernel(*refs, apply_ln, blocks_per_tile):
    if apply_ln:
        x_ref, lng_ref, lnb_ref, *refs = refs
    else:
        x_ref, *refs = refs
    (w1_ref, w2_ref, w3_ref, qg_ref, kvg_ref, wuq_ref, wuqs_ref, wukv_ref, pk_ref, mqa_ref, mka_ref,
     cq_ref, sq_ref, ck_ref, sk_ref, cr_ref, sr_ref, *outs) = refs
    if apply_ln:
        x0_ref, *outs = outs
    (qm_ref, km_ref, vm_ref, rq_ref, rk_ref, rv_ref, rg_ref, mq_ref, mk_ref, mv_ref, kmean_ref) = outs

    x = x_ref[...]
    if apply_ln:
        x = _layer_norm(x, lng_ref[...], lnb_ref[...])
        x0_ref[...] = x
    xb = x.astype(BF16)

    ha = _dot(xb, w1_ref[...])
    qn = _rms_norm(ha[:, :MLA_Q_RANK], qg_ref[...]).astype(BF16)
    q = _dot(qn, wuq_ref[...]) * cq_ref[...] + _dot(qn, wuqs_ref[...]) * sq_ref[...]
    qm_ref[...] = q.astype(BF16)
    kvn = _rms_norm(ha[:, MLA_Q_RANK:MLA_Q_RANK + MLA_KV_RANK], kvg_ref[...]).astype(BF16)
    kv = _dot(kvn, wukv_ref[...])
    o = MLA_Q_RANK + MLA_KV_RANK
    kpe = ha[:, o:o + LANES] * ck_ref[...] + ha[:, o + LANES:o + 2 * LANES] * sk_ref[...]
    k_width = MLA_HEADS * MLA_SLOT
    km_ref[...] = (kv[:, :k_width] + _dot(kpe.astype(BF16), pk_ref[...])).astype(BF16)
    vm_ref[...] = kv[:, k_width:].T.astype(BF16)

    hb = _dot(xb, w2_ref[...])
    P = RET_PAD
    rq = hb[:, 0:P] * cr_ref[...] + hb[:, P:2 * P] * sr_ref[...]
    rk = (hb[:, 2 * P:3 * P] * cr_ref[...] + hb[:, 3 * P:4 * P] * sr_ref[...]) * (RET_DK ** -0.5)
    rq_ref[...] = rq[:, :W_RET].astype(BF16)
    rk_ref[...] = rk[:, :W_RET].astype(BF16)
    rv_ref[...] = hb[:, 4 * P:5 * P].T.astype(BF16)
    rg_ref[...] = hb[:, 5 * P:5 * P + W_RET]

    hc = _dot(xb, w3_ref[...])
    P = MOBA_SLOTS
    mq_ref[...] = (hc[:, :P] * (MOBA_DH ** -0.5 * LOG2E) + mqa_ref[...]).astype(BF16)
    mk = hc[:, P:2 * P]
    mk_ref[...] = (mk + mka_ref[...]).astype(BF16)
    mv_ref[...] = hc[:, 2 * P:].T.astype(BF16)
    for bi in range(blocks_per_tile):
        kmean_ref[bi] = jnp.mean(mk[bi * MOBA_BLOCK:(bi + 1) * MOBA_BLOCK], axis=0, keepdims=True)


def _in_proj(x, ln, weights, tables, *, seq, tm):
    T, D = x.shape
    n_tiles = T // tm
    tiles_per_seq = seq // tm
    bpt = tm // MOBA_BLOCK
    apply_ln = ln is not None
    row = lambda i: (i, 0)
    const = lambda i: (0, 0)
    pos = lambda i: (i % tiles_per_seq, 0)

    in_specs, args = [pl.BlockSpec((tm, D), row)], [x]
    if apply_ln:
        in_specs += [pl.BlockSpec((1, D), const)] * 2
        args += list(ln)
    for w in weights:
        in_specs.append(pl.BlockSpec(w.shape, const))
        args.append(w)
    for t in tables:
        in_specs.append(pl.BlockSpec((tm, t.shape[1]), pos))
        args.append(t)

    def out(width, dtype):
        return jax.ShapeDtypeStruct((T, width), dtype), pl.BlockSpec((tm, width), row)

    outs = []
    if apply_ln:
        outs.append(out(D, F32))
    def out_t(height, dtype):
        return jax.ShapeDtypeStruct((height, T), dtype), pl.BlockSpec((height, tm), lambda i: (0, i))

    outs += [out(MLA_HEADS * MLA_SLOT, BF16), out(MLA_HEADS * MLA_SLOT, BF16), out_t(W_MLA_OUT, BF16),
             out(W_RET, BF16), out(W_RET, BF16), out_t(RET_PAD, BF16), out(W_RET, F32),
             out(MOBA_SLOTS, BF16), out(MOBA_SLOTS, BF16), out_t(MOBA_PAD, BF16)]
    outs.append((jax.ShapeDtypeStruct((T // MOBA_BLOCK, 1, MOBA_SLOTS), F32),
                 pl.BlockSpec((bpt, 1, MOBA_SLOTS), lambda i: (i, 0, 0))))
    return pl.pallas_call(
        functools.partial(_in_proj_kernel, apply_ln=apply_ln, blocks_per_tile=bpt),
        grid=(n_tiles,),
        in_specs=in_specs,
        out_specs=[o[1] for o in outs],
        out_shape=[o[0] for o in outs],
        compiler_params=pltpu.CompilerParams(dimension_semantics=("parallel",)),
        name="in_proj_ln" if apply_ln else "in_proj",
    )(*args)


SUM_ROWS = 16


def _softmax_step(s, vt, m, acc, query_bias=None):
    smax = jnp.max(s, 0, keepdims=True)
    if query_bias is not None:
        smax = smax + query_bias
    m_new = jnp.maximum(m, smax)
    shift = m_new if query_bias is None else m_new - query_bias
    p = jnp.exp2(s - shift).astype(BF16)
    vt_ones = jnp.concatenate([vt, jnp.ones((SUM_ROWS, vt.shape[1]), BF16)], axis=0)
    acc = jnp.exp2(m - m_new) * acc + _dot(vt_ones, p)
    return m_new, acc


def _softmax_init(n_heads, dv, tq):
    return tuple((jnp.full((1, tq), NEG, F32), jnp.zeros((dv + SUM_ROWS, tq), F32)) for _ in range(n_heads))


def _softmax_finish(carry, dv):
    return [acc[:dv] / acc[dv:dv + 1] for _, acc in carry]


def _attend_blocks(i, n_heads, init, score, update):
    def run(blocks, carry):
        pending = [score(blocks[0][0], h) for h in range(n_heads)]
        for n, (j, diagonal) in enumerate(blocks):
            following, new = [], []
            for h in range(n_heads):
                new.append(update(j, h, pending[h], carry[h], diagonal))
                if n + 1 < len(blocks):
                    following.append(score(blocks[n + 1][0], h))
            carry, pending = tuple(new), following
        return carry

    carry = run([(i, True)], init)
    carry = lax.cond(i % 2 == 1, lambda c: run([(i - 1, False)], c), lambda c: c, carry)
    return lax.fori_loop(0, i // 2, lambda t, c: run([(2 * t, False), (2 * t + 1, False)], c), carry)


def _causal_mask(tq):
    keys = lax.broadcasted_iota(jnp.int32, (tq, tq), 0)
    queries = lax.broadcasted_iota(jnp.int32, (tq, tq), 1)
    return keys <= queries


def _mla_attn_kernel(q_ref, k_ref, vt_ref, o_ref, *, tq):
    i = pl.program_id(1)
    causal = _causal_mask(tq)
    slots = [slice(MLA_SLOT * h, MLA_SLOT * (h + 1)) for h in range(MLA_HEADS)]

    def keys(j):
        return pl.ds(pl.multiple_of(j * tq, tq), tq)

    def score(j, h):
        return _dot_nt(k_ref[keys(j), slots[h]], q_ref[:, slots[h]])

    def update(j, h, s, state, diagonal):
        s = jnp.where(causal, s, NEG) if diagonal else s
        return _softmax_step(s, vt_ref[MLA_V * h:MLA_V * (h + 1), keys(j)], *state)

    carry = _attend_blocks(i, MLA_HEADS, _softmax_init(MLA_HEADS, MLA_V, tq), score, update)
    out = jnp.concatenate(_softmax_finish(carry, MLA_V), axis=0)
    o_ref[...] = out.T


def _mla_attention(q, k, vt, *, batch, seq, tq):
    T = q.shape[0]
    nq = seq // tq
    return pl.pallas_call(
        functools.partial(_mla_attn_kernel, tq=tq),
        grid=(batch, nq),
        in_specs=[pl.BlockSpec((tq, q.shape[1]), lambda b, i: (b * nq + i, 0)),
                  pl.BlockSpec((seq, k.shape[1]), lambda b, i: (b, 0)),
                  pl.BlockSpec((vt.shape[0], seq), lambda b, i: (0, b))],
        out_specs=pl.BlockSpec((tq, W_MLA_OUT), lambda b, i: (b * nq + i, 0)),
        out_shape=jax.ShapeDtypeStruct((T, W_MLA_OUT), F32),
        compiler_params=pltpu.CompilerParams(dimension_semantics=("parallel", "arbitrary")),
        name="mla_attention",
    )(q, k, vt)


SLOPE_PARTS = 3


def _moba_slopes_log2():
    return tuple(float(np.sum(_slope_parts(h), dtype=np.float32)) for h in range(MOBA_HEADS))


def _slope_parts(h):
    rest = np.float32(2.0 ** (-8.0 * (h + 1.0) / MOBA_HEADS) * LOG2E)
    parts = []
    for _ in range(SLOPE_PARTS):
        piece = np.float32(rest.astype(BF16))
        parts.append(piece)
        rest = np.float32(rest - piece)
    return np.asarray(parts, np.float32)


def _moba_alibi_lanes(rows):
    q_add = np.zeros((1, MOBA_SLOTS), np.float32)
    k_add = np.zeros((rows, MOBA_SLOTS), np.float32)
    offset = (np.arange(rows) % MOBA_BLOCK).astype(np.float32)
    for h in range(MOBA_HEADS):
        lanes = slice(h * MOBA_SLOT + MOBA_DH, h * MOBA_SLOT + MOBA_DH + SLOPE_PARTS)
        q_add[0, lanes] = _slope_parts(h)
        k_add[:, lanes] = offset[:, None]
    return jnp.asarray(q_add), jnp.asarray(k_add)


def _moba_kernel(q_ref, k_ref, vt_ref, kmean_ref, o_ref, bias_ref, *, n_blocks, slopes):
    tq = MOBA_BLOCK
    i = pl.program_id(1)
    causal = _causal_mask(tq)
    blk_id = lax.broadcasted_iota(jnp.int32, (n_blocks, tq), 0)
    past = blk_id < i

    slots = [slice(MOBA_SLOT * h, MOBA_SLOT * (h + 1)) for h in range(MOBA_HEADS)]
    for h in range(MOBA_HEADS):
        q = q_ref[:, slots[h]]
        km = kmean_ref[:, 0, slots[h]]
        km_hi = km.astype(BF16)
        km_lo = (km - km_hi.astype(F32)).astype(BF16)
        gate = _dot_nt(km_hi, q) + _dot_nt(km_lo, q)
        gate = jnp.where(past, gate, -jnp.inf)
        rank = jnp.zeros((n_blocks, tq), F32)
        for mb in range(n_blocks):
            gm = gate[mb:mb + 1, :]
            beats = (gm > gate) | ((gm == gate) & (blk_id > mb))
            rank = rank + jnp.where(beats, 1.0, 0.0)
        chosen = past & (rank < MOBA_TOPK)
        bias = jnp.where(chosen, 0.0, NEG)
        for nb in range(n_blocks):
            bias_ref[h, nb] = bias[nb:nb + 1, :]

    def keys(j):
        return pl.ds(pl.multiple_of(j * tq, tq), tq)

    def score(j, h):
        return _dot_nt(k_ref[keys(j), slots[h]], q_ref[:, slots[h]])

    def update(j, h, s, state, diagonal):
        vt = vt_ref[MOBA_DH * h:MOBA_DH * (h + 1), keys(j)]
        if diagonal:
            return _softmax_step(jnp.where(causal, s, NEG), vt, *state)
        query_bias = bias_ref[h, j] + slopes[h] * ((j - i) * tq).astype(F32)
        return _softmax_step(s, vt, *state, query_bias=query_bias)

    carry = _attend_blocks(i, MOBA_HEADS, _softmax_init(MOBA_HEADS, MOBA_DH, tq), score, update)
    pad = jnp.zeros((MOBA_PAD - W_MOBA, tq), F32)
    out = jnp.concatenate(_softmax_finish(carry, MOBA_DH) + [pad], axis=0)
    o_ref[...] = out.T[:, :W_MOBA]


def _moba_attention(q, k, vt, kmean, *, batch, seq):
    T = q.shape[0]
    tq = MOBA_BLOCK
    nq = seq // tq
    slopes = _moba_slopes_log2()
    return pl.pallas_call(
        functools.partial(_moba_kernel, n_blocks=nq, slopes=slopes),
        grid=(batch, nq),
        in_specs=[pl.BlockSpec((tq, MOBA_SLOTS), lambda b, i: (b * nq + i, 0)),
                  pl.BlockSpec((seq, MOBA_SLOTS), lambda b, i: (b, 0)),
                  pl.BlockSpec((vt.shape[0], seq), lambda b, i: (0, b)),
                  pl.BlockSpec((nq, 1, MOBA_SLOTS), lambda b, i: (b, 0, 0))],
        out_specs=pl.BlockSpec((tq, W_MOBA), lambda b, i: (b * nq + i, 0)),
        out_shape=jax.ShapeDtypeStruct((T, W_MOBA), F32),
        scratch_shapes=[pltpu.VMEM((MOBA_HEADS, nq, 1, tq), F32)],
        compiler_params=pltpu.CompilerParams(dimension_semantics=("parallel", "arbitrary")),
        name="moba_attention",
    )(q, k, vt, kmean)


RET_TILE = 256


def _retention_kernel(q_ref, k_ref, vt_ref, g_ref, din_ref, qdec_ref, kdec_ref, o_ref, state_ref, *, chunk_decay):
    @pl.when(pl.program_id(1) == 0)
    def _():
        state_ref[...] = jnp.zeros_like(state_ref)

    C = q_ref.shape[0]
    heads = [slice(RET_DK * h, RET_DK * (h + 1)) for h in range(RET_HEADS)]
    kd = (k_ref[...].astype(F32) * kdec_ref[...]).astype(BF16)
    scores = [_dot_nt(k_ref[:, hs], q_ref[:, hs]) for hs in heads]
    outs = []
    for h, hs in enumerate(heads):
        vt = vt_ref[hs, :]
        state = state_ref[h]
        o = _dot(vt, (scores[h] * din_ref[h]).astype(BF16))
        o = o + _dot_nt(state.astype(BF16), q_ref[:, hs]) * qdec_ref[h]
        state_ref[h] = state * chunk_decay[h] + _dot(vt, kd[:, hs])
        oc = o - jnp.mean(o, 0, keepdims=True)
        outs.append(oc * lax.rsqrt(jnp.mean(oc * oc, 0, keepdims=True) + RMS_EPS))
    outs.append(jnp.zeros((RET_PAD - W_RET, C), F32))
    normed = jnp.concatenate(outs, axis=0).T[:, :W_RET]
    o_ref[...] = (_silu(g_ref[...]) * normed).astype(BF16)


def _retention(q, k, vt, g, *, batch, seq):
    T = q.shape[0]
    C = RET_TILE
    nc = seq // C
    log_gamma = np.log(1.0 - 2.0 ** (-5.0 - np.arange(RET_HEADS, dtype=np.float64)))
    idx = np.arange(C, dtype=np.float64)
    lag = idx[None, :] - idx[:, None]
    decay_in = np.where(lag >= 0, np.exp(log_gamma[:, None, None] * np.maximum(lag, 0.0)), 0.0)
    q_decay = np.exp(log_gamma[:, None, None] * (idx[None, None, :] + 1.0))
    k_decay = np.repeat(np.exp(log_gamma[None, :] * (C - 1.0 - idx[:, None])), RET_DK, axis=1)
    chunk_decay = tuple(float(c) for c in np.exp(log_gamma * C))
    tile = lambda b, c: (b * nc + c, 0)
    const3 = lambda b, c: (0, 0, 0)
    return pl.pallas_call(
        functools.partial(_retention_kernel, chunk_decay=chunk_decay),
        grid=(batch, nc),
        in_specs=[pl.BlockSpec((C, W_RET), tile), pl.BlockSpec((C, W_RET), tile),
                  pl.BlockSpec((vt.shape[0], C), lambda b, c: (0, b * nc + c)), pl.BlockSpec((C, W_RET), tile),
                  pl.BlockSpec((RET_HEADS, C, C), const3), pl.BlockSpec((RET_HEADS, 1, C), const3),
                  pl.BlockSpec((C, W_RET), lambda b, c: (0, 0))],
        out_specs=pl.BlockSpec((C, W_RET), tile),
        out_shape=jax.ShapeDtypeStruct((T, W_RET), BF16),
        scratch_shapes=[pltpu.VMEM((RET_HEADS, RET_DV, RET_DK), F32)],
        compiler_params=pltpu.CompilerParams(dimension_semantics=("parallel", "arbitrary")),
        name="retention",
    )(q, k, vt, g, jnp.asarray(decay_in, F32), jnp.asarray(q_decay, F32), jnp.asarray(k_decay, F32))


def _out_proj_kernel(oa_ref, ob_ref, oc_ref, x_ref, ba_ref, bc_ref, wa_ref, wb_ref, wc_ref, g_ref, b_ref,
                     y_ref, *maybe_tiles_ref, alpha):
    na = _rms_norm(oa_ref[...], ba_ref[...]).astype(BF16)
    nc = _rms_norm(oc_ref[...], bc_ref[...]).astype(BF16)
    mix = _dot(na, wa_ref[...]) + _dot(ob_ref[...], wb_ref[...]) + _dot(nc, wc_ref[...])
    y = _layer_norm(alpha * x_ref[...] + mix, g_ref[...], b_ref[...])
    y_ref[...] = y
    for tiles_ref in maybe_tiles_ref:
        for c in range(SUBLANES):
            tiles_ref[pl.ds(c, y.shape[0], stride=SUBLANES), :] = y[:, c * LANES:(c + 1) * LANES]


def _out_proj(oa, ob, oc, x, beta_a, beta_c, wa, wb, wc, g, b, *, alpha, tm, with_row_tiles):
    T, D = x.shape
    row = lambda i: (i, 0)
    const = lambda i: (0, 0)
    full = lambda a: pl.BlockSpec(a.shape, const)
    out_specs = [pl.BlockSpec((tm, D), row)]
    out_shape = [jax.ShapeDtypeStruct((T, D), F32)]
    if with_row_tiles:
        assert D == SUBLANES * LANES
        out_specs.append(pl.BlockSpec((tm * SUBLANES, LANES), row))
        out_shape.append(jax.ShapeDtypeStruct((T * SUBLANES, LANES), F32))
    return pl.pallas_call(
        functools.partial(_out_proj_kernel, alpha=alpha),
        grid=(T // tm,),
        in_specs=[pl.BlockSpec((tm, oa.shape[1]), row), pl.BlockSpec((tm, ob.shape[1]), row),
                  pl.BlockSpec((tm, oc.shape[1]), row), pl.BlockSpec((tm, D), row),
                  full(beta_a), full(beta_c), full(wa), full(wb), full(wc), full(g), full(b)],
        out_specs=out_specs,
        out_shape=out_shape,
        compiler_params=pltpu.CompilerParams(dimension_semantics=("parallel",)),
        name="out_proj_tiles" if with_row_tiles else "out_proj",
    )(oa, ob, oc, x, beta_a, beta_c, wa, wb, wc, g, b)


def _ffn_kernel(x_ref, wg_ref, wu_ref, wd_ref, g_ref, b_ref, y_ref, xb_ref, acc_ref, *, alpha):
    j = pl.program_id(1)

    @pl.when(j == 0)
    def _():
        xb_ref[...] = x_ref[...].astype(BF16)
        acc_ref[...] = jnp.zeros_like(acc_ref)

    xb = xb_ref[...]
    h = _silu(_dot(xb, wg_ref[...])) * _dot(xb, wu_ref[...])
    acc_ref[...] += _dot(h.astype(BF16), wd_ref[...])

    @pl.when(j == pl.num_programs(1) - 1)
    def _():
        y_ref[...] = _layer_norm(alpha * x_ref[...] + acc_ref[...], g_ref[...], b_ref[...])


def _ffn(x, wg, wu, wd, g, b, *, alpha, tm, tf):
    T, D = x.shape
    F = wg.shape[1]
    return pl.pallas_call(
        functools.partial(_ffn_kernel, alpha=alpha),
        grid=(T // tm, F // tf),
        in_specs=[pl.BlockSpec((tm, D), lambda i, j: (i, 0)),
                  pl.BlockSpec((D, tf), lambda i, j: (0, j)),
                  pl.BlockSpec((D, tf), lambda i, j: (0, j)),
                  pl.BlockSpec((tf, D), lambda i, j: (j, 0)),
                  pl.BlockSpec((1, D), lambda i, j: (0, 0)),
                  pl.BlockSpec((1, D), lambda i, j: (0, 0))],
        out_specs=pl.BlockSpec((tm, D), lambda i, j: (i, 0)),
        out_shape=jax.ShapeDtypeStruct((T, D), F32),
        scratch_shapes=[pltpu.VMEM((tm, D), BF16), pltpu.VMEM((tm, D), F32)],
        compiler_params=pltpu.CompilerParams(dimension_semantics=("parallel", "arbitrary")),
        name="ffn",
    )(x, wg, wu, wd, g, b)


SUBLANES = 8
R_E1, R_E2, R_W1, R_W2 = 0, 1, 2, 3


def _lane_pick(x, lane, idx):
    return jnp.sum(jnp.where(lane == idx, x, 0.0), -1, keepdims=True)


def _router_kernel(x_ref, rhi_ref, rlo_ref, route_ref, sel_ref):
    x = x_ref[...]
    x_hi = x.astype(BF16)
    x_lo = (x - x_hi.astype(F32)).astype(BF16)
    logits = _dot(x_hi, rhi_ref[...]) + _dot(x_lo, rhi_ref[...]) + _dot(x_hi, rlo_ref[...])
    lane = lax.broadcasted_iota(jnp.int32, logits.shape, 1)
    logits = jnp.where(lane < N_EXPERTS, logits, -jnp.inf)
    m1 = jnp.max(logits, -1, keepdims=True)
    i1 = jnp.min(jnp.where(logits == m1, lane, LANES), -1, keepdims=True)
    rest = jnp.where(lane == i1, -jnp.inf, logits)
    m2 = jnp.max(rest, -1, keepdims=True)
    i2 = jnp.min(jnp.where(rest == m2, lane, LANES), -1, keepdims=True)
    e2 = jnp.exp(m2 - m1)
    w1 = 1.0 / (1.0 + e2)
    w2 = e2 / (1.0 + e2)
    route = jnp.where(lane == R_E1, i1.astype(F32), 0.0) + jnp.where(lane == R_E2, i2.astype(F32), 0.0)
    route_ref[...] = route + jnp.where(lane == R_W1, w1, 0.0) + jnp.where(lane == R_W2, w2, 0.0)
    sel_ref[...] = jnp.where((lane == i1) | (lane == i2), 1.0, 0.0).astype(BF16)


def _router(x, r_hi, r_lo, *, tm):
    T, D = x.shape
    return pl.pallas_call(
        _router_kernel,
        grid=(T // tm,),
        in_specs=[pl.BlockSpec((tm, D), lambda i: (i, 0)),
                  pl.BlockSpec((D, LANES), lambda i: (0, 0)),
                  pl.BlockSpec((D, LANES), lambda i: (0, 0))],
        out_specs=[pl.BlockSpec((tm, LANES), lambda i: (i, 0)), pl.BlockSpec((tm, LANES), lambda i: (i, 0))],
        out_shape=[jax.ShapeDtypeStruct((T, LANES), F32), jax.ShapeDtypeStruct((T, LANES), BF16)],
        compiler_params=pltpu.CompilerParams(dimension_semantics=("parallel",)),
        name="router",
    )(x, r_hi, r_lo)


def _rank_kernel(sel_ref, rank_ref, count_ref, carry_ref):
    @pl.when(pl.program_id(0) == 0)
    def _():
        carry_ref[...] = jnp.zeros_like(carry_ref)

    sel = sel_ref[...]
    tm = sel.shape[0]
    earlier = lax.broadcasted_iota(jnp.int32, (tm, tm), 1) < lax.broadcasted_iota(jnp.int32, (tm, tm), 0)
    rank_ref[...] = _dot(jnp.where(earlier, 1.0, 0.0).astype(BF16), sel) + carry_ref[...]
    carry_ref[...] += jnp.sum(sel.astype(F32), 0, keepdims=True)
    count_ref[...] = carry_ref[...]


def _rank(sel, *, tm):
    T = sel.shape[0]
    return pl.pallas_call(
        _rank_kernel,
        grid=(T // tm,),
        in_specs=[pl.BlockSpec((tm, LANES), lambda i: (i, 0))],
        out_specs=[pl.BlockSpec((tm, LANES), lambda i: (i, 0)), pl.BlockSpec((1, LANES), lambda i: (0, 0))],
        out_shape=[jax.ShapeDtypeStruct((T, LANES), F32), jax.ShapeDtypeStruct((1, LANES), F32)],
        scratch_shapes=[pltpu.VMEM((1, LANES), F32)],
        compiler_params=pltpu.CompilerParams(dimension_semantics=("arbitrary",)),
        name="moe_rank",
    )(sel)


def _position_kernel(route_ref, rank_ref, off_ref, pos_ref):
    route = route_ref[...]
    lane = lax.broadcasted_iota(jnp.int32, route.shape, 1)
    e1 = _lane_pick(route, lane, R_E1).astype(jnp.int32)
    e2 = _lane_pick(route, lane, R_E2).astype(jnp.int32)
    slot = rank_ref[...] + off_ref[...]
    p1 = _lane_pick(slot, lane, e1)
    p2 = _lane_pick(slot, lane, e2)
    pos_ref[...] = (jnp.where(lane == 0, p1, 0.0) + jnp.where(lane == 1, p2, 0.0)).astype(jnp.int32)


def _positions(route, rank, offsets, *, tm):
    T = route.shape[0]
    return pl.pallas_call(
        _position_kernel,
        grid=(T // tm,),
        in_specs=[pl.BlockSpec((tm, LANES), lambda i: (i, 0)), pl.BlockSpec((tm, LANES), lambda i: (i, 0)),
                  pl.BlockSpec((1, LANES), lambda i: (0, 0))],
        out_specs=pl.BlockSpec((tm, LANES), lambda i: (i, 0)),
        out_shape=jax.ShapeDtypeStruct((T, LANES), jnp.int32),
        compiler_params=pltpu.CompilerParams(dimension_semantics=("parallel",)),
        name="moe_positions",
    )(route, rank, offsets)


def _row_copy(src_ref, src_row, dst_ref, dst_row, sem):
    return pltpu.make_async_copy(
        src_ref.at[pl.ds(pl.multiple_of(src_row * SUBLANES, SUBLANES), SUBLANES), :],
        dst_ref.at[pl.ds(pl.multiple_of(dst_row * SUBLANES, SUBLANES), SUBLANES), :],
        sem)


def _start_then_wait(n_rows, copies):
    def start(r, c):
        for cp in copies(r):
            cp.start()
        return c

    def wait(r, c):
        for cp in copies(r):
            cp.wait()
        return c

    lax.fori_loop(0, n_rows, start, 0)
    return lambda: lax.fori_loop(0, n_rows, wait, 0)


def _dispatch_kernel(p1_ref, p2_ref, x_ref, buf_in_ref, buf_ref, sem):
    del buf_in_ref
    wait_all = _start_then_wait(p1_ref.shape[-1], lambda r: (
        _row_copy(x_ref, r, buf_ref, p1_ref[0, r], sem), _row_copy(x_ref, r, buf_ref, p2_ref[0, r], sem)))
    wait_all()


def _dispatch(x_tiles, p1, p2, n_rows, *, tm):
    T = x_tiles.shape[0] // SUBLANES
    buf = jnp.zeros((n_rows * SUBLANES, LANES), F32)
    smem = lambda: pl.BlockSpec((None, 1, tm), lambda i: (i, 0, 0), memory_space=pltpu.SMEM)
    return pl.pallas_call(
        _dispatch_kernel,
        grid=(T // tm,),
        in_specs=[smem(), smem(), pl.BlockSpec((tm * SUBLANES, LANES), lambda i: (i, 0)),
                  pl.BlockSpec(memory_space=pl.ANY)],
        out_specs=pl.BlockSpec(memory_space=pl.ANY),
        out_shape=jax.ShapeDtypeStruct(buf.shape, F32),
        input_output_aliases={3: 0},
        scratch_shapes=[pltpu.SemaphoreType.DMA(())],
        compiler_params=pltpu.CompilerParams(dimension_semantics=("arbitrary",)),
        name="moe_dispatch",
    )(p1, p2, x_tiles, buf)


def _rows_from_tiles(ref, c, rows):
    return ref[pl.ds(c, rows, stride=SUBLANES), :]


def _expert_kernel(tile_expert_ref, n_used_ref, x_ref, wg_ref, wu_ref, wd_ref, y_ref, xb_ref, acc_ref):
    del tile_expert_ref
    g = pl.program_id(0)
    j = pl.program_id(1)
    tm = xb_ref.shape[0]
    used = g < n_used_ref[0]

    @pl.when(used & (j == 0))
    def _():
        for c in range(SUBLANES):
            xb_ref[:, c * LANES:(c + 1) * LANES] = _rows_from_tiles(x_ref, c, tm).astype(BF16)
        acc_ref[...] = jnp.zeros_like(acc_ref)

    @pl.when(used)
    def _():
        xb = xb_ref[...]
        h = _silu(_dot(xb, wg_ref[...])) * _dot(xb, wu_ref[...])
        acc_ref[...] += _dot(h.astype(BF16), wd_ref[...])

    @pl.when(j == pl.num_programs(1) - 1)
    def _():
        for c in range(SUBLANES):
            cols = acc_ref[:, c * LANES:(c + 1) * LANES]
            y_ref[pl.ds(c, tm, stride=SUBLANES), :] = jnp.where(used, cols, 0.0)


def _experts(buf, tile_expert, n_used, wg, wu, wd, *, tm, tf):
    n_tiles = buf.shape[0] // (tm * SUBLANES)
    E, D, F = wg.shape
    x_map = lambda g, j, te, nu: (g, 0)
    grid_spec = pltpu.PrefetchScalarGridSpec(
        num_scalar_prefetch=2,
        grid=(n_tiles, F // tf),
        in_specs=[pl.BlockSpec((tm * SUBLANES, LANES), x_map),
                  pl.BlockSpec((None, D, tf), lambda g, j, te, nu: (te[g], 0, j)),
                  pl.BlockSpec((None, D, tf), lambda g, j, te, nu: (te[g], 0, j)),
                  pl.BlockSpec((None, tf, D), lambda g, j, te, nu: (te[g], j, 0))],
        out_specs=pl.BlockSpec((tm * SUBLANES, LANES), x_map),
        scratch_shapes=[pltpu.VMEM((tm, D), BF16), pltpu.VMEM((tm, D), F32)],
    )
    return pl.pallas_call(
        _expert_kernel,
        grid_spec=grid_spec,
        out_shape=jax.ShapeDtypeStruct(buf.shape, F32),
        compiler_params=pltpu.CompilerParams(dimension_semantics=("parallel", "arbitrary")),
        name="moe_experts",
    )(tile_expert, n_used, buf, wg, wu, wd)


def _combine_kernel(p1_ref, p2_ref, x_ref, route_ref, y_hbm_ref, g_ref, b_ref, out_ref, y1_ref, y2_ref, sem, *, alpha):
    tm = x_ref.shape[0]
    wait_all = _start_then_wait(tm, lambda r: (
        _row_copy(y_hbm_ref, p1_ref[0, r], y1_ref, r, sem), _row_copy(y_hbm_ref, p2_ref[0, r], y2_ref, r, sem)))
    route = route_ref[...]
    lane = lax.broadcasted_iota(jnp.int32, route.shape, 1)
    w1 = _lane_pick(route, lane, R_W1)
    w2 = _lane_pick(route, lane, R_W2)
    wait_all()
    f = jnp.concatenate([w1 * _rows_from_tiles(y1_ref, c, tm) + w2 * _rows_from_tiles(y2_ref, c, tm)
                         for c in range(SUBLANES)], axis=1)
    out_ref[...] = _layer_norm(alpha * x_ref[...] + f, g_ref[...], b_ref[...])


def _combine(x, route, y_tiles, p1, p2, g, b, *, alpha, tm):
    T, D = x.shape
    smem = lambda: pl.BlockSpec((None, 1, tm), lambda i: (i, 0, 0), memory_space=pltpu.SMEM)
    return pl.pallas_call(
        functools.partial(_combine_kernel, alpha=alpha),
        grid=(T // tm,),
        in_specs=[smem(), smem(), pl.BlockSpec((tm, D), lambda i: (i, 0)), pl.BlockSpec((tm, LANES), lambda i: (i, 0)),
                  pl.BlockSpec(memory_space=pl.ANY),
                  pl.BlockSpec((1, D), lambda i: (0, 0)), pl.BlockSpec((1, D), lambda i: (0, 0))],
        out_specs=pl.BlockSpec((tm, D), lambda i: (i, 0)),
        out_shape=jax.ShapeDtypeStruct((T, D), F32),
        scratch_shapes=[pltpu.VMEM((tm * SUBLANES, LANES), F32), pltpu.VMEM((tm * SUBLANES, LANES), F32),
                        pltpu.SemaphoreType.DMA(())],
        compiler_params=pltpu.CompilerParams(dimension_semantics=("arbitrary",)),
        name="moe_combine",
    )(p1, p2, x, route, y_tiles, g, b)


def _tile_plan(counts, n_tiles, tm):
    padded = ((counts + tm - 1) // tm) * tm
    ends = jnp.cumsum(padded)
    offsets = ends - padded
    tile_start = jnp.arange(n_tiles, dtype=jnp.int32) * tm
    tile_expert = jnp.minimum(jnp.sum(tile_start[:, None] >= ends[None, :], axis=1), N_EXPERTS - 1).astype(jnp.int32)
    n_used = (ends[-1] // tm).astype(jnp.int32).reshape(1)
    tile_expert = jnp.where(jnp.arange(n_tiles) < n_used[0], tile_expert, tile_expert[jnp.maximum(n_used[0] - 1, 0)])
    return offsets, tile_expert, n_used


def _moe(x, x_tiles, router_w, wg, wu, wd, g, b, *, alpha, tm, tf):
    T, D = x.shape
    assert D == SUBLANES * LANES
    r = router_w.astype(F32)
    r_hi = r.astype(BF16)
    r_lo = (r - r_hi.astype(F32)).astype(BF16)
    route, sel = _router(x, _pad_cols(r_hi, LANES), _pad_cols(r_lo, LANES), tm=tm)
    rank, counts = _rank(sel, tm=tm)
    n_tiles = (2 * T) // tm + N_EXPERTS
    offsets, tile_expert, n_used = _tile_plan(counts[0, :N_EXPERTS].astype(jnp.int32), n_tiles, tm)
    pos = _positions(route, rank, _pad_cols(offsets.astype(F32).reshape(1, -1), LANES), tm=tm)
    p1 = pos[:, 0].reshape(T // tm, 1, tm)
    p2 = pos[:, 1].reshape(T // tm, 1, tm)
    buf = _dispatch(x_tiles, p1, p2, n_tiles * tm, tm=tm)
    y_tiles = _experts(buf, tile_expert, n_used, wg, wu, wd, tm=tm, tf=tf)
    return _combine(x, route, y_tiles, p1, p2, g, b, alpha=alpha, tm=tm)


def _pad_cols(w, width):
    return jnp.pad(w, ((0, 0), (0, width - w.shape[1])))


def _swap_halves(w, head_dim):
    d = w.shape[1]
    idx = np.arange(d).reshape(d // head_dim, 2, head_dim // 2)[:, ::-1, :].reshape(d)
    return w[:, idx]


def _prep_in_weights(w_in, w_uq, w_ukv):
    sizes = (MLA_Q_RANK, MLA_KV_RANK, MLA_ROPE, W_RET, W_RET, W_RET, W_RET, W_MOBA, W_MOBA, W_MOBA)
    offs = np.cumsum((0,) + sizes)
    c_q, c_kv, k_rope, rq, rk, rv, rg, mq, mk, mv = (w_in[:, offs[n]:offs[n + 1]] for n in range(len(sizes)))
    w1 = jnp.concatenate([c_q, c_kv, _pad_cols(k_rope, LANES), _pad_cols(_swap_halves(k_rope, MLA_ROPE), LANES)], 1)
    w2 = jnp.concatenate([_pad_cols(w, RET_PAD) for w in
                          (rq, _swap_halves(rq, RET_DK), rk, _swap_halves(rk, RET_DK), rv, rg)], 1)
    def head_slots(w):
        w = w.reshape(w.shape[0], MOBA_HEADS, MOBA_DH)
        return jnp.pad(w, ((0, 0), (0, 0), (0, MOBA_SLOT - MOBA_DH))).reshape(w.shape[0], MOBA_SLOTS)

    w3 = jnp.concatenate([head_slots(mq), head_slots(mk), _pad_cols(mv, MOBA_PAD)], 1)

    dq = MLA_NOPE + MLA_ROPE
    uq = w_uq.reshape(MLA_Q_RANK, MLA_HEADS, dq)
    uq_rope_sw = _swap_halves(uq[:, :, MLA_NOPE:].reshape(MLA_Q_RANK, -1), MLA_ROPE).reshape(MLA_Q_RANK, MLA_HEADS, MLA_ROPE)
    zpad = jnp.zeros((MLA_Q_RANK, MLA_HEADS, MLA_SLOT - dq), w_uq.dtype)
    wuq = jnp.concatenate([uq, zpad], 2).reshape(MLA_Q_RANK, -1)
    wuq_sw = jnp.concatenate([jnp.zeros_like(uq[:, :, :MLA_NOPE]), uq_rope_sw, zpad], 2).reshape(MLA_Q_RANK, -1)
    ukv = w_ukv.reshape(MLA_KV_RANK, MLA_HEADS, MLA_NOPE + MLA_V)
    k_part = jnp.concatenate([ukv[:, :, :MLA_NOPE], jnp.zeros((MLA_KV_RANK, MLA_HEADS, MLA_SLOT - MLA_NOPE), w_ukv.dtype)], 2)
    wukv = jnp.concatenate([k_part.reshape(MLA_KV_RANK, -1), ukv[:, :, MLA_NOPE:].reshape(MLA_KV_RANK, -1)], 1)
    return tuple(w.astype(BF16) for w in (w1, w2, w3)), tuple(w.astype(BF16) for w in (wuq, wuq_sw, wukv))


def _rope_key_placement():
    p = np.zeros((LANES, MLA_HEADS * MLA_SLOT), np.float32)
    for h in range(MLA_HEADS):
        for j in range(MLA_ROPE):
            p[j, h * MLA_SLOT + MLA_NOPE + j] = 1.0
    return jnp.asarray(p, BF16)


def _rope_tables(seq):
    def cos_sin(dim):
        pos = jnp.arange(seq, dtype=F32)
        inv = ROPE_THETA ** (-jnp.arange(0, dim, 2, dtype=F32) / dim)
        ang = pos[:, None] * inv[None, :]
        c, s = jnp.cos(ang), jnp.sin(ang)
        return jnp.concatenate([c, c], 1), jnp.concatenate([-s, s], 1)

    ca, sa = cos_sin(MLA_ROPE)
    cb, sb = cos_sin(RET_DK)
    scale = (MLA_NOPE + MLA_ROPE) ** -0.5 * LOG2E
    ones = jnp.ones((seq, MLA_NOPE), F32)
    zeros_n = jnp.zeros((seq, MLA_NOPE), F32)
    zpad = jnp.zeros((seq, MLA_SLOT - MLA_NOPE - MLA_ROPE), F32)
    cq = jnp.tile(jnp.concatenate([ones, ca, zpad], 1), (1, MLA_HEADS)) * scale
    sq = jnp.tile(jnp.concatenate([zeros_n, sa, zpad], 1), (1, MLA_HEADS)) * scale
    ck = _pad_cols(ca, LANES)
    sk = _pad_cols(sa, LANES)
    cr = _pad_cols(jnp.tile(cb, (1, RET_HEADS)), RET_PAD)
    sr = _pad_cols(jnp.tile(sb, (1, RET_HEADS)), RET_PAD)
    return cq, sq, ck, sk, cr, sr


def _ff_tile(width, preferred):
    return preferred if width % preferred == 0 else width


def kernel(x, ln_emb_g, ln_emb_b, w_in, q_norm_g, kv_norm_g, w_uq, w_ukv, beta_mla, beta_moba, w_o, ln1_g, ln1_b,
           ffn_w_gate, ffn_w_up, ffn_w_down, router, exp_w_gate, exp_w_up, exp_w_down, ln2_g, ln2_b):
    B, S, D = x.shape
    depth = w_in.shape[0]
    alpha = float((2 * depth) ** 0.25)
    assert S % MOBA_BLOCK == 0 and S % RET_CHUNK == 0
    tm = 512 if S % 512 == 0 else MOBA_BLOCK
    row = lambda v: v.reshape(1, -1).astype(F32)

    tables = _rope_tables(S)
    pk = _rope_key_placement()
    alibi_lanes = _moba_alibi_lanes(tm)
    xs = x.reshape(B * S, D)
    for l in range(depth):
        w123, wmla = _prep_in_weights(w_in[l], w_uq[l], w_ukv[l])
        weights = (*w123, row(q_norm_g[l]), row(kv_norm_g[l]), *wmla, pk, *alibi_lanes)
        ln = (row(ln_emb_g), row(ln_emb_b)) if l == 0 else None
        outs = _in_proj(xs, ln, weights, tables, seq=S, tm=tm)
        if l == 0:
            xs, *outs = outs
        qm, km, vm, rq, rk, rv, rg, mq, mk, mv, kmean = outs
        o_a = _mla_attention(qm, km, vm, batch=B, seq=S, tq=MOBA_BLOCK)
        o_b = _retention(rq, rk, rv, rg, batch=B, seq=S)
        o_c = _moba_attention(mq, mk, mv, kmean, batch=B, seq=S)
        wo = w_o[l].astype(BF16)
        routed = l % 2 == 1
        xs, *x_tiles = _out_proj(o_a, o_b, o_c, xs, row(beta_mla[l]), row(beta_moba[l]),
                                 wo[:W_MLA_OUT], wo[W_MLA_OUT:W_MLA_OUT + W_RET], wo[W_MLA_OUT + W_RET:],
                                 row(ln1_g[l]), row(ln1_b[l]), alpha=alpha, tm=tm, with_row_tiles=routed)
        j = l // 2
        if routed:
            xs = _moe(xs, *x_tiles, router[j], exp_w_gate[j].astype(BF16), exp_w_up[j].astype(BF16),
                      exp_w_down[j].astype(BF16), row(ln2_g[l]), row(ln2_b[l]), alpha=alpha, tm=tm,
                      tf=_ff_tile(exp_w_gate.shape[-1], 896))
        else:
            xs = _ffn(xs, ffn_w_gate[j].astype(BF16), ffn_w_up[j].astype(BF16), ffn_w_down[j].astype(BF16),
                      row(ln2_g[l]), row(ln2_b[l]), alpha=alpha, tm=tm, tf=_ff_tile(ffn_w_gate.shape[-1], 1408))
    return xs.reshape(B, S, D)
```

```python
import functools

import numpy as np
import jax
import jax.numpy as jnp
from jax import lax
from jax.experimental import pallas as pl
from jax.experimental.pallas import tpu as pltpu

F32 = jnp.float32
BF16 = jnp.bfloat16

MLA_HEADS, MLA_NOPE, MLA_ROPE, MLA_V = 6, 64, 32, 64
MLA_Q_RANK, MLA_KV_RANK = 256, 128
RET_HEADS, RET_DK, RET_DV, RET_CHUNK = 5, 64, 64, 128
MOBA_HEADS, MOBA_DH, MOBA_BLOCK, MOBA_TOPK = 5, 64, 256, 3
ROPE_THETA = 10000.0
LN_EPS = 1e-5
RMS_EPS = 1e-6
N_EXPERTS = 8
W_MLA_OUT = MLA_HEADS * MLA_V
W_RET = RET_HEADS * RET_DK
W_MOBA = MOBA_HEADS * MOBA_DH

LANES = 128
MLA_SLOT = 128
NEG = -1e30


def _pad_to(n, m):
    return -(-n // m) * m


RET_PAD = _pad_to(W_RET, LANES)
MOBA_PAD = _pad_to(W_MOBA, LANES)
MOBA_SLOT = 128
MOBA_SLOTS = MOBA_HEADS * MOBA_SLOT
LOG2E = 1.4426950408889634


def _layer_norm(x, g, b):
    mu = jnp.mean(x, -1, keepdims=True)
    xc = x - mu
    var = jnp.mean(xc * xc, -1, keepdims=True)
    return xc * lax.rsqrt(var + LN_EPS) * g + b


def _rms_norm(x, g):
    return x * lax.rsqrt(jnp.mean(x * x, -1, keepdims=True) + RMS_EPS) * g


def _silu(x):
    return x / (1.0 + jnp.exp(-x))


def _dot(a, b):
    return jnp.dot(a, b, preferred_element_type=F32)


def _dot_nt(a, b):
    return lax.dot_general(a, b, (((1,), (1,)), ((), ())), preferred_element_type=F32)


def _in_proj_kernel(*refs, apply_ln, blocks_per_tile):
    if apply_ln:
        x_ref, lng_ref, lnb_ref, *refs = refs
    else:
        x_ref, *refs = refs
    (w1_ref, w2_ref, w3_ref, qg_ref, kvg_ref, wuq_ref, wuqs_ref, wukv_ref, pk_ref, mqa_ref, mka_ref,
     cq_ref, sq_ref, ck_ref, sk_ref, cr_ref, sr_ref, *outs) = refs
    if apply_ln:
        x0_ref, *outs = outs
    (qm_ref, km_ref, vm_ref, rq_ref, rk_ref, rv_ref, rg_ref, mq_ref, mk_ref, mv_ref, kmean_ref) = outs

    x = x_ref[...]
    if apply_ln:
        x = _layer_norm(x, lng_ref[...], lnb_ref[...])
        x0_ref[...] = x
    xb = x.astype(BF16)

    ha = _dot(xb, w1_ref[...])
    qn = _rms_norm(ha[:, :MLA_Q_RANK], qg_ref[...]).astype(BF16)
    q = _dot(qn, wuq_ref[...]) * cq_ref[...] + _dot(qn, wuqs_ref[...]) * sq_ref[...]
    qm_ref[...] = q.astype(BF16)
    kvn = _rms_norm(ha[:, MLA_Q_RANK:MLA_Q_RANK + MLA_KV_RANK], kvg_ref[...]).astype(BF16)
    kv = _dot(kvn, wukv_ref[...])
    o = MLA_Q_RANK + MLA_KV_RANK
    kpe = ha[:, o:o + LANES] * ck_ref[...] + ha[:, o + LANES:o + 2 * LANES] * sk_ref[...]
    k_width = MLA_HEADS * MLA_SLOT
    km_ref[...] = (kv[:, :k_width] + _dot(kpe.astype(BF16), pk_ref[...])).astype(BF16)
    vm_ref[...] = kv[:, k_width:].T.astype(BF16)

    hb = _dot(xb, w2_ref[...])
    P = RET_PAD
    rq = hb[:, 0:P] * cr_ref[...] + hb[:, P:2 * P] * sr_ref[...]
    rk = (hb[:, 2 * P:3 * P] * cr_ref[...] + hb[:, 3 * P:4 * P] * sr_ref[...]) * (RET_DK ** -0.5)
    rq_ref[...] = rq[:, :W_RET].astype(BF16)
    rk_ref[...] = rk[:, :W_RET].astype(BF16)
    rv_ref[...] = hb[:, 4 * P:5 * P].T.astype(BF16)
    rg_ref[...] = hb[:, 5 * P:5 * P + W_RET]

    hc = _dot(xb, w3_ref[...])
    P = MOBA_SLOTS
    mq_ref[...] = (hc[:, :P] * (MOBA_DH ** -0.5 * LOG2E) + mqa_ref[...]).astype(BF16)
    mk = hc[:, P:2 * P]
    mk_ref[...] = (mk + mka_ref[...]).astype(BF16)
    mv_ref[...] = hc[:, 2 * P:].T.astype(BF16)
    for bi in range(blocks_per_tile):
        kmean_ref[bi] = jnp.mean(mk[bi * MOBA_BLOCK:(bi + 1) * MOBA_BLOCK], axis=0, keepdims=True)


def _in_proj(x, ln, weights, tables, *, seq, tm):
    T, D = x.shape
    n_tiles = T // tm
    tiles_per_seq = seq // tm
    bpt = tm // MOBA_BLOCK
    apply_ln = ln is not None
    row = lambda i: (i, 0)
    const = lambda i: (0, 0)
    pos = lambda i: (i % tiles_per_seq, 0)

    in_specs, args = [pl.BlockSpec((tm, D), row)], [x]
    if apply_ln:
        in_specs += [pl.BlockSpec((1, D), const)] * 2
        args += list(ln)
    for w in weights:
        in_specs.append(pl.BlockSpec(w.shape, const))
        args.append(w)
    for t in tables:
        in_specs.append(pl.BlockSpec((tm, t.shape[1]), pos))
        args.append(t)

    def out(width, dtype):
        return jax.ShapeDtypeStruct((T, width), dtype), pl.BlockSpec((tm, width), row)

    outs = []
    if apply_ln:
        outs.append(out(D, F32))
    def out_t(height, dtype):
        return jax.ShapeDtypeStruct((height, T), dtype), pl.BlockSpec((height, tm), lambda i: (0, i))

    outs += [out(MLA_HEADS * MLA_SLOT, BF16), out(MLA_HEADS * MLA_SLOT, BF16), out_t(W_MLA_OUT, BF16),
             out(W_RET, BF16), out(W_RET, BF16), out_t(RET_PAD, BF16), out(W_RET, F32),
             out(MOBA_SLOTS, BF16), out(MOBA_SLOTS, BF16), out_t(MOBA_PAD, BF16)]
    outs.append((jax.ShapeDtypeStruct((T // MOBA_BLOCK, 1, MOBA_SLOTS), F32),
                 pl.BlockSpec((bpt, 1, MOBA_SLOTS), lambda i: (i, 0, 0))))
    return pl.pallas_call(
        functools.partial(_in_proj_kernel, apply_ln=apply_ln, blocks_per_tile=bpt),
        grid=(n_tiles,),
        in_specs=in_specs,
        out_specs=[o[1] for o in outs],
        out_shape=[o[0] for o in outs],
        compiler_params=pltpu.CompilerParams(dimension_semantics=("parallel",)),
        name="in_proj_ln" if apply_ln else "in_proj",
    )(*args)


SUM_ROWS = 16


def _softmax_step(s, vt, m, acc, query_bias=None):
    smax = jnp.max(s, 0, keepdims=True)
    if query_bias is not None:
        smax = smax + query_bias
    m_new = jnp.maximum(m, smax)
    shift = m_new if query_bias is None else m_new - query_bias
    p = jnp.exp2(s - shift).astype(BF16)
    vt_ones = jnp.concatenate([vt, jnp.ones((SUM_ROWS, vt.shape[1]), BF16)], axis=0)
    acc = jnp.exp2(m - m_new) * acc + _dot(vt_ones, p)
    return m_new, acc


def _softmax_init(n_heads, dv, tq):
    return tuple((jnp.full((1, tq), NEG, F32), jnp.zeros((dv + SUM_ROWS, tq), F32)) for _ in range(n_heads))


def _softmax_finish(carry, dv):
    return [acc[:dv] / acc[dv:dv + 1] for _, acc in carry]


def _attend_blocks(i, n_heads, init, score, update):
    def run(blocks, carry):
        pending = [score(blocks[0][0], h) for h in range(n_heads)]
        for n, (j, diagonal) in enumerate(blocks):
            following, new = [], []
            for h in range(n_heads):
                new.append(update(j, h, pending[h], carry[h], diagonal))
                if n + 1 < len(blocks):
                    following.append(score(blocks[n + 1][0], h))
            carry, pending = tuple(new), following
        return carry

    carry = run([(i, True)], init)
    carry = lax.cond(i % 2 == 1, lambda c: run([(i - 1, False)], c), lambda c: c, carry)
    return lax.fori_loop(0, i // 2, lambda t, c: run([(2 * t, False), (2 * t + 1, False)], c), carry)


def _causal_mask(tq):
    keys = lax.broadcasted_iota(jnp.int32, (tq, tq), 0)
    queries = lax.broadcasted_iota(jnp.int32, (tq, tq), 1)
    return keys <= queries


def _mla_attn_kernel(q_ref, k_ref, vt_ref, o_ref, *, tq):
    i = pl.program_id(1)
    causal = _causal_mask(tq)
    slots = [slice(MLA_SLOT * h, MLA_SLOT * (h + 1)) for h in range(MLA_HEADS)]

    def keys(j):
        return pl.ds(pl.multiple_of(j * tq, tq), tq)

    def score(j, h):
        return _dot_nt(k_ref[keys(j), slots[h]], q_ref[:, slots[h]])

    def update(j, h, s, state, diagonal):
        s = jnp.where(causal, s, NEG) if diagonal else s
        return _softmax_step(s, vt_ref[MLA_V * h:MLA_V * (h + 1), keys(j)], *state)

    carry = _attend_blocks(i, MLA_HEADS, _softmax_init(MLA_HEADS, MLA_V, tq), score, update)
    out = jnp.concatenate(_softmax_finish(carry, MLA_V), axis=0)
    o_ref[...] = out.T


def _mla_attention(q, k, vt, *, batch, seq, tq):
    T = q.shape[0]
    nq = seq // tq
    return pl.pallas_call(
        functools.partial(_mla_attn_kernel, tq=tq),
        grid=(batch, nq),
        in_specs=[pl.BlockSpec((tq, q.shape[1]), lambda b, i: (b * nq + i, 0)),
                  pl.BlockSpec((seq, k.shape[1]), lambda b, i: (b, 0)),
                  pl.BlockSpec((vt.shape[0], seq), lambda b, i: (0, b))],
        out_specs=pl.BlockSpec((tq, W_MLA_OUT), lambda b, i: (b * nq + i, 0)),
        out_shape=jax.ShapeDtypeStruct((T, W_MLA_OUT), F32),
        compiler_params=pltpu.CompilerParams(dimension_semantics=("parallel", "arbitrary")),
        name="mla_attention",
    )(q, k, vt)


SLOPE_PARTS = 3


def _moba_slopes_log2():
    return tuple(float(np.sum(_slope_parts(h), dtype=np.float32)) for h in range(MOBA_HEADS))


def _slope_parts(h):
    rest = np.float32(2.0 ** (-8.0 * (h + 1.0) / MOBA_HEADS) * LOG2E)
    parts = []
    for _ in range(SLOPE_PARTS):
        piece = np.float32(rest.astype(BF16))
        parts.append(piece)
        rest = np.float32(rest - piece)
    return np.asarray(parts, np.float32)


def _moba_alibi_lanes(rows):
    q_add = np.zeros((1, MOBA_SLOTS), np.float32)
    k_add = np.zeros((rows, MOBA_SLOTS), np.float32)
    offset = (np.arange(rows) % MOBA_BLOCK).astype(np.float32)
    for h in range(MOBA_HEADS):
        lanes = slice(h * MOBA_SLOT + MOBA_DH, h * MOBA_SLOT + MOBA_DH + SLOPE_PARTS)
        q_add[0, lanes] = _slope_parts(h)
        k_add[:, lanes] = offset[:, None]
    return jnp.asarray(q_add), jnp.asarray(k_add)


def _moba_kernel(q_ref, k_ref, vt_ref, kmean_ref, o_ref, bias_ref, *, n_blocks, slopes):
    tq = MOBA_BLOCK
    i = pl.program_id(1)
    causal = _causal_mask(tq)
    blk_id = lax.broadcasted_iota(jnp.int32, (n_blocks, tq), 0)
    past = blk_id < i

    slots = [slice(MOBA_SLOT * h, MOBA_SLOT * (h + 1)) for h in range(MOBA_HEADS)]
    for h in range(MOBA_HEADS):
        q = q_ref[:, slots[h]]
        km = kmean_ref[:, 0, slots[h]]
        km_hi = km.astype(BF16)
        km_lo = (km - km_hi.astype(F32)).astype(BF16)
        gate = _dot_nt(km_hi, q) + _dot_nt(km_lo, q)
        gate = jnp.where(past, gate, -jnp.inf)
        rank = jnp.zeros((n_blocks, tq), F32)
        for mb in range(n_blocks):
            gm = gate[mb:mb + 1, :]
            beats = (gm > gate) | ((gm == gate) & (blk_id > mb))
            rank = rank + jnp.where(beats, 1.0, 0.0)
        chosen = past & (rank < MOBA_TOPK)
        bias = jnp.where(chosen, 0.0, NEG)
        for nb in range(n_blocks):
            bias_ref[h, nb] = bias[nb:nb + 1, :]

    def keys(j):
        return pl.ds(pl.multiple_of(j * tq, tq), tq)

    def score(j, h):
        return _dot_nt(k_ref[keys(j), slots[h]], q_ref[:, slots[h]])

    def update(j, h, s, state, diagonal):
        vt = vt_ref[MOBA_DH * h:MOBA_DH * (h + 1), keys(j)]
        if diagonal:
            return _softmax_step(jnp.where(causal, s, NEG), vt, *state)
        query_bias = bias_ref[h, j] + slopes[h] * ((j - i) * tq).astype(F32)
        return _softmax_step(s, vt, *state, query_bias=query_bias)

    carry = _attend_blocks(i, MOBA_HEADS, _softmax_init(MOBA_HEADS, MOBA_DH, tq), score, update)
    pad = jnp.zeros((MOBA_PAD - W_MOBA, tq), F32)
    out = jnp.concatenate(_softmax_finish(carry, MOBA_DH) + [pad], axis=0)
    o_ref[...] = out.T[:, :W_MOBA]


def _moba_attention(q, k, vt, kmean, *, batch, seq):
    T = q.shape[0]
    tq = MOBA_BLOCK
    nq = seq // tq
    slopes = _moba_slopes_log2()
    return pl.pallas_call(
        functools.partial(_moba_kernel, n_blocks=nq, slopes=slopes),
        grid=(batch, nq),
        in_specs=[pl.BlockSpec((tq, MOBA_SLOTS), lambda b, i: (b * nq + i, 0)),
                  pl.BlockSpec((seq, MOBA_SLOTS), lambda b, i: (b, 0)),
                  pl.BlockSpec((vt.shape[0], seq), lambda b, i: (0, b)),
                  pl.BlockSpec((nq, 1, MOBA_SLOTS), lambda b, i: (b, 0, 0))],
        out_specs=pl.BlockSpec((tq, W_MOBA), lambda b, i: (b * nq + i, 0)),
        out_shape=jax.ShapeDtypeStruct((T, W_MOBA), F32),
        scratch_shapes=[pltpu.VMEM((MOBA_HEADS, nq, 1, tq), F32)],
        compiler_params=pltpu.CompilerParams(dimension_semantics=("parallel", "arbitrary")),
        name="moba_attention",
    )(q, k, vt, kmean)


RET_TILE = 256


def _retention_kernel(q_ref, k_ref, vt_ref, g_ref, din_ref, qdec_ref, kdec_ref, o_ref, state_ref, *, chunk_decay):
    @pl.when(pl.program_id(1) == 0)
    def _():
        state_ref[...] = jnp.zeros_like(state_ref)

    C = q_ref.shape[0]
    heads = [slice(RET_DK * h, RET_DK * (h + 1)) for h in range(RET_HEADS)]
    kd = (k_ref[...].astype(F32) * kdec_ref[...]).astype(BF16)
    scores = [_dot_nt(k_ref[:, hs], q_ref[:, hs]) for hs in heads]
    outs = []
    for h, hs in enumerate(heads):
        vt = vt_ref[hs, :]
        state = state_ref[h]
        o = _dot(vt, (scores[h] * din_ref[h]).astype(BF16))
        o = o + _dot_nt(state.astype(BF16), q_ref[:, hs]) * qdec_ref[h]
        state_ref[h] = state * chunk_decay[h] + _dot(vt, kd[:, hs])
        oc = o - jnp.mean(o, 0, keepdims=True)
        outs.append(oc * lax.rsqrt(jnp.mean(oc * oc, 0, keepdims=True) + RMS_EPS))
    outs.append(jnp.zeros((RET_PAD - W_RET, C), F32))
    normed = jnp.concatenate(outs, axis=0).T[:, :W_RET]
    o_ref[...] = (_silu(g_ref[...]) * normed).astype(BF16)


def _retention(q, k, vt, g, *, batch, seq):
    T = q.shape[0]
    C = RET_TILE
    nc = seq // C
    log_gamma = np.log(1.0 - 2.0 ** (-5.0 - np.arange(RET_HEADS, dtype=np.float64)))
    idx = np.arange(C, dtype=np.float64)
    lag = idx[None, :] - idx[:, None]
    decay_in = np.where(lag >= 0, np.exp(log_gamma[:, None, None] * np.maximum(lag, 0.0)), 0.0)
    q_decay = np.exp(log_gamma[:, None, None] * (idx[None, None, :] + 1.0))
    k_decay = np.repeat(np.exp(log_gamma[None, :] * (C - 1.0 - idx[:, None])), RET_DK, axis=1)
    chunk_decay = tuple(float(c) for c in np.exp(log_gamma * C))
    tile = lambda b, c: (b * nc + c, 0)
    const3 = lambda b, c: (0, 0, 0)
    return pl.pallas_call(
        functools.partial(_retention_kernel, chunk_decay=chunk_decay),
        grid=(batch, nc),
        in_specs=[pl.BlockSpec((C, W_RET), tile), pl.BlockSpec((C, W_RET), tile),
                  pl.BlockSpec((vt.shape[0], C), lambda b, c: (0, b * nc + c)), pl.BlockSpec((C, W_RET), tile),
                  pl.BlockSpec((RET_HEADS, C, C), const3), pl.BlockSpec((RET_HEADS, 1, C), const3),
                  pl.BlockSpec((C, W_RET), lambda b, c: (0, 0))],
        out_specs=pl.BlockSpec((C, W_RET), tile),
        out_shape=jax.ShapeDtypeStruct((T, W_RET), BF16),
        scratch_shapes=[pltpu.VMEM((RET_HEADS, RET_DV, RET_DK), F32)],
        compiler_params=pltpu.CompilerParams(dimension_semantics=("parallel", "arbitrary")),
        name="retention",
    )(q, k, vt, g, jnp.asarray(decay_in, F32), jnp.asarray(q_decay, F32), jnp.asarray(k_decay, F32))


def _out_proj_kernel(oa_ref, ob_ref, oc_ref, x_ref, ba_ref, bc_ref, wa_ref, wb_ref, wc_ref, g_ref, b_ref,
                     y_ref, *maybe_tiles_ref, alpha):
    na = _rms_norm(oa_ref[...], ba_ref[...]).astype(BF16)
    nc = _rms_norm(oc_ref[...], bc_ref[...]).astype(BF16)
    mix = _dot(na, wa_ref[...]) + _dot(ob_ref[...], wb_ref[...]) + _dot(nc, wc_ref[...])
    y = _layer_norm(alpha * x_ref[...] + mix, g_ref[...], b_ref[...])
    y_ref[...] = y
    for tiles_ref in maybe_tiles_ref:
        for c in range(SUBLANES):
            tiles_ref[pl.ds(c, y.shape[0], stride=SUBLANES), :] = y[:, c * LANES:(c + 1) * LANES]


def _out_proj(oa, ob, oc, x, beta_a, beta_c, wa, wb, wc, g, b, *, alpha, tm, with_row_tiles):
    T, D = x.shape
    row = lambda i: (i, 0)
    const = lambda i: (0, 0)
    full = lambda a: pl.BlockSpec(a.shape, const)
    out_specs = [pl.BlockSpec((tm, D), row)]
    out_shape = [jax.ShapeDtypeStruct((T, D), F32)]
    if with_row_tiles:
        assert D == SUBLANES * LANES
        out_specs.append(pl.BlockSpec((tm * SUBLANES, LANES), row))
        out_shape.append(jax.ShapeDtypeStruct((T * SUBLANES, LANES), F32))
    return pl.pallas_call(
        functools.partial(_out_proj_kernel, alpha=alpha),
        grid=(T // tm,),
        in_specs=[pl.BlockSpec((tm, oa.shape[1]), row), pl.BlockSpec((tm, ob.shape[1]), row),
                  pl.BlockSpec((tm, oc.shape[1]), row), pl.BlockSpec((tm, D), row),
                  full(beta_a), full(beta_c), full(wa), full(wb), full(wc), full(g), full(b)],
        out_specs=out_specs,
        out_shape=out_shape,
        compiler_params=pltpu.CompilerParams(dimension_semantics=("parallel",)),
        name="out_proj_tiles" if with_row_tiles else "out_proj",
    )(oa, ob, oc, x, beta_a, beta_c, wa, wb, wc, g, b)


def _ffn_kernel(x_ref, wg_ref, wu_ref, wd_ref, g_ref, b_ref, y_ref, xb_ref, acc_ref, *, alpha):
    j = pl.program_id(1)

    @pl.when(j == 0)
    def _():
        xb_ref[...] = x_ref[...].astype(BF16)
        acc_ref[...] = jnp.zeros_like(acc_ref)

    xb = xb_ref[...]
    h = _silu(_dot(xb, wg_ref[...])) * _dot(xb, wu_ref[...])
    acc_ref[...] += _dot(h.astype(BF16), wd_ref[...])

    @pl.when(j == pl.num_programs(1) - 1)
    def _():
        y_ref[...] = _layer_norm(alpha * x_ref[...] + acc_ref[...], g_ref[...], b_ref[...])


def _ffn(x, wg, wu, wd, g, b, *, alpha, tm):
    T, D = x.shape
    nj, _, tf = wg.shape
    return pl.pallas_call(
        functools.partial(_ffn_kernel, alpha=alpha),
        grid=(T // tm, nj),
        in_specs=[pl.BlockSpec((tm, D), lambda i, j: (i, 0)),
                  pl.BlockSpec((None, D, tf), lambda i, j: (j, 0, 0)),
                  pl.BlockSpec((None, D, tf), lambda i, j: (j, 0, 0)),
                  pl.BlockSpec((tf, D), lambda i, j: (j, 0)),
                  pl.BlockSpec((1, D), lambda i, j: (0, 0)),
                  pl.BlockSpec((1, D), lambda i, j: (0, 0))],
        out_specs=pl.BlockSpec((tm, D), lambda i, j: (i, 0)),
        out_shape=jax.ShapeDtypeStruct((T, D), F32),
        scratch_shapes=[pltpu.VMEM((tm, D), BF16), pltpu.VMEM((tm, D), F32)],
        compiler_params=pltpu.CompilerParams(dimension_semantics=("parallel", "arbitrary")),
        name="ffn",
    )(x, wg, wu, wd, g, b)


SUBLANES = 8
R_E1, R_E2, R_W1, R_W2 = 0, 1, 2, 3


def _lane_pick(x, lane, idx):
    return jnp.sum(jnp.where(lane == idx, x, 0.0), -1, keepdims=True)


def _router_kernel(x_ref, rhi_ref, rlo_ref, route_ref, sel_ref):
    x = x_ref[...]
    x_hi = x.astype(BF16)
    x_lo = (x - x_hi.astype(F32)).astype(BF16)
    logits = _dot(x_hi, rhi_ref[...]) + _dot(x_lo, rhi_ref[...]) + _dot(x_hi, rlo_ref[...])
    lane = lax.broadcasted_iota(jnp.int32, logits.shape, 1)
    logits = jnp.where(lane < N_EXPERTS, logits, -jnp.inf)
    m1 = jnp.max(logits, -1, keepdims=True)
    i1 = jnp.min(jnp.where(logits == m1, lane, LANES), -1, keepdims=True)
    rest = jnp.where(lane == i1, -jnp.inf, logits)
    m2 = jnp.max(rest, -1, keepdims=True)
    i2 = jnp.min(jnp.where(rest == m2, lane, LANES), -1, keepdims=True)
    e2 = jnp.exp(m2 - m1)
    w1 = 1.0 / (1.0 + e2)
    w2 = e2 / (1.0 + e2)
    route = jnp.where(lane == R_E1, i1.astype(F32), 0.0) + jnp.where(lane == R_E2, i2.astype(F32), 0.0)
    route_ref[...] = route + jnp.where(lane == R_W1, w1, 0.0) + jnp.where(lane == R_W2, w2, 0.0)
    sel_ref[...] = jnp.where((lane == i1) | (lane == i2), 1.0, 0.0).astype(BF16)


def _router(x, r_hi, r_lo, *, tm):
    T, D = x.shape
    return pl.pallas_call(
        _router_kernel,
        grid=(T // tm,),
        in_specs=[pl.BlockSpec((tm, D), lambda i: (i, 0)),
                  pl.BlockSpec((D, LANES), lambda i: (0, 0)),
                  pl.BlockSpec((D, LANES), lambda i: (0, 0))],
        out_specs=[pl.BlockSpec((tm, LANES), lambda i: (i, 0)), pl.BlockSpec((tm, LANES), lambda i: (i, 0))],
        out_shape=[jax.ShapeDtypeStruct((T, LANES), F32), jax.ShapeDtypeStruct((T, LANES), BF16)],
        compiler_params=pltpu.CompilerParams(dimension_semantics=("parallel",)),
        name="router",
    )(x, r_hi, r_lo)


def _rank_kernel(sel_ref, rank_ref, count_ref, carry_ref):
    @pl.when(pl.program_id(0) == 0)
    def _():
        carry_ref[...] = jnp.zeros_like(carry_ref)

    sel = sel_ref[...]
    tm = sel.shape[0]
    earlier = lax.broadcasted_iota(jnp.int32, (tm, tm), 1) < lax.broadcasted_iota(jnp.int32, (tm, tm), 0)
    rank_ref[...] = _dot(jnp.where(earlier, 1.0, 0.0).astype(BF16), sel) + carry_ref[...]
    carry_ref[...] += jnp.sum(sel.astype(F32), 0, keepdims=True)
    count_ref[...] = carry_ref[...]


def _rank(sel, *, tm):
    T = sel.shape[0]
    return pl.pallas_call(
        _rank_kernel,
        grid=(T // tm,),
        in_specs=[pl.BlockSpec((tm, LANES), lambda i: (i, 0))],
        out_specs=[pl.BlockSpec((tm, LANES), lambda i: (i, 0)), pl.BlockSpec((1, LANES), lambda i: (0, 0))],
        out_shape=[jax.ShapeDtypeStruct((T, LANES), F32), jax.ShapeDtypeStruct((1, LANES), F32)],
        scratch_shapes=[pltpu.VMEM((1, LANES), F32)],
        compiler_params=pltpu.CompilerParams(dimension_semantics=("arbitrary",)),
        name="moe_rank",
    )(sel)


def _position_kernel(route_ref, rank_ref, off_ref, pos_ref):
    route = route_ref[...]
    lane = lax.broadcasted_iota(jnp.int32, route.shape, 1)
    e1 = _lane_pick(route, lane, R_E1).astype(jnp.int32)
    e2 = _lane_pick(route, lane, R_E2).astype(jnp.int32)
    slot = rank_ref[...] + off_ref[...]
    p1 = _lane_pick(slot, lane, e1)
    p2 = _lane_pick(slot, lane, e2)
    pos_ref[...] = (jnp.where(lane == 0, p1, 0.0) + jnp.where(lane == 1, p2, 0.0)).astype(jnp.int32)


def _positions(route, rank, offsets, *, tm):
    T = route.shape[0]
    return pl.pallas_call(
        _position_kernel,
        grid=(T // tm,),
        in_specs=[pl.BlockSpec((tm, LANES), lambda i: (i, 0)), pl.BlockSpec((tm, LANES), lambda i: (i, 0)),
                  pl.BlockSpec((1, LANES), lambda i: (0, 0))],
        out_specs=pl.BlockSpec((tm, LANES), lambda i: (i, 0)),
        out_shape=jax.ShapeDtypeStruct((T, LANES), jnp.int32),
        compiler_params=pltpu.CompilerParams(dimension_semantics=("parallel",)),
        name="moe_positions",
    )(route, rank, offsets)


def _row_copy(src_ref, src_row, dst_ref, dst_row, sem):
    return pltpu.make_async_copy(
        src_ref.at[pl.ds(pl.multiple_of(src_row * SUBLANES, SUBLANES), SUBLANES), :],
        dst_ref.at[pl.ds(pl.multiple_of(dst_row * SUBLANES, SUBLANES), SUBLANES), :],
        sem)


def _start_rows(n_rows, copies):
    def start(r, c):
        for n, cp in enumerate(copies(r)):
            cp.start(priority=n % 2)
        return c

    lax.fori_loop(0, n_rows, start, 0)


def _wait_rows(n_rows, copies):
    def wait(r, c):
        for cp in copies(r):
            cp.wait()
        return c

    lax.fori_loop(0, n_rows, wait, 0)


def _dispatch_kernel(p1_ref, p2_ref, p1_prev_ref, p2_prev_ref, x_ref, buf_in_ref, buf_ref, stage_ref, sems):
    del buf_in_ref
    i = pl.program_id(0)
    slot = i % 2
    tm = p1_ref.shape[-1]

    def copies(slot, a_ref, b_ref):
        stage, sem = stage_ref.at[slot], sems.at[slot]
        return lambda r: (_row_copy(stage, r, buf_ref, a_ref[0, r], sem), _row_copy(stage, r, buf_ref, b_ref[0, r], sem))

    stage_ref[slot] = x_ref[...]
    _start_rows(tm, copies(slot, p1_ref, p2_ref))

    @pl.when(i > 0)
    def _():
        _wait_rows(tm, copies(1 - slot, p1_prev_ref, p2_prev_ref))

    @pl.when(i == pl.num_programs(0) - 1)
    def _():
        _wait_rows(tm, copies(slot, p1_ref, p2_ref))


def _index_blocks(tm, n_tiles, shift):
    return pl.BlockSpec((None, 1, tm), lambda i: (jnp.clip(i + shift, 0, n_tiles - 1), 0, 0), memory_space=pltpu.SMEM)


def _dispatch(x_tiles, p1, p2, n_rows, *, tm):
    T = x_tiles.shape[0] // SUBLANES
    n_tiles = T // tm
    buf = jnp.zeros((n_rows * SUBLANES, LANES), F32)
    cur, prev = _index_blocks(tm, n_tiles, 0), _index_blocks(tm, n_tiles, -1)
    return pl.pallas_call(
        _dispatch_kernel,
        grid=(n_tiles,),
        in_specs=[cur, cur, prev, prev, pl.BlockSpec((tm * SUBLANES, LANES), lambda i: (i, 0)),
                  pl.BlockSpec(memory_space=pl.ANY)],
        out_specs=pl.BlockSpec(memory_space=pl.ANY),
        out_shape=jax.ShapeDtypeStruct(buf.shape, F32),
        input_output_aliases={5: 0},
        scratch_shapes=[pltpu.VMEM((2, tm * SUBLANES, LANES), F32), pltpu.SemaphoreType.DMA((2,))],
        compiler_params=pltpu.CompilerParams(dimension_semantics=("arbitrary",)),
        name="moe_dispatch",
    )(p1, p2, p1, p2, x_tiles, buf)


def _rows_from_tiles(ref, c, rows):
    return ref[pl.ds(c, rows, stride=SUBLANES), :]


def _expert_kernel(tile_expert_ref, n_used_ref, x_ref, wg_ref, wu_ref, wd_ref, y_ref, xb_ref, acc_ref):
    del tile_expert_ref
    g = pl.program_id(0)
    j = pl.program_id(1)
    tm = xb_ref.shape[0]
    used = g < n_used_ref[0]

    @pl.when(used & (j == 0))
    def _():
        for c in range(SUBLANES):
            xb_ref[:, c * LANES:(c + 1) * LANES] = _rows_from_tiles(x_ref, c, tm).astype(BF16)
        acc_ref[...] = jnp.zeros_like(acc_ref)

    @pl.when(used)
    def _():
        xb = xb_ref[...]
        h = _silu(_dot(xb, wg_ref[...])) * _dot(xb, wu_ref[...])
        acc_ref[...] += _dot(h.astype(BF16), wd_ref[...])

    @pl.when(j == pl.num_programs(1) - 1)
    def _():
        for c in range(SUBLANES):
            cols = acc_ref[:, c * LANES:(c + 1) * LANES]
            y_ref[pl.ds(c, tm, stride=SUBLANES), :] = jnp.where(used, cols, 0.0)


def _experts(buf, tile_expert, n_used, wg, wu, wd, *, tm):
    n_tiles = buf.shape[0] // (tm * SUBLANES)
    E, nj, D, tf = wg.shape
    x_map = lambda g, j, te, nu: (g, 0)
    grid_spec = pltpu.PrefetchScalarGridSpec(
        num_scalar_prefetch=2,
        grid=(n_tiles, nj),
        in_specs=[pl.BlockSpec((tm * SUBLANES, LANES), x_map),
                  pl.BlockSpec((None, None, D, tf), lambda g, j, te, nu: (te[g], j, 0, 0)),
                  pl.BlockSpec((None, None, D, tf), lambda g, j, te, nu: (te[g], j, 0, 0)),
                  pl.BlockSpec((None, tf, D), lambda g, j, te, nu: (te[g], j, 0))],
        out_specs=pl.BlockSpec((tm * SUBLANES, LANES), x_map),
        scratch_shapes=[pltpu.VMEM((tm, D), BF16), pltpu.VMEM((tm, D), F32)],
    )
    return pl.pallas_call(
        _expert_kernel,
        grid_spec=grid_spec,
        out_shape=jax.ShapeDtypeStruct(buf.shape, F32),
        compiler_params=pltpu.CompilerParams(dimension_semantics=("parallel", "arbitrary")),
        name="moe_experts",
    )(tile_expert, n_used, buf, wg, wu, wd)


def _combine_kernel(p1_ref, p2_ref, p1_next_ref, p2_next_ref, x_ref, route_ref, y_hbm_ref, g_ref, b_ref, out_ref,
                    y1_ref, y2_ref, sems, *, alpha):
    i = pl.program_id(0)
    slot = i % 2
    tm = x_ref.shape[0]

    def copies(slot, a_ref, b_ref):
        y1, y2, sem = y1_ref.at[slot], y2_ref.at[slot], sems.at[slot]
        return lambda r: (_row_copy(y_hbm_ref, a_ref[0, r], y1, r, sem), _row_copy(y_hbm_ref, b_ref[0, r], y2, r, sem))

    @pl.when(i == 0)
    def _():
        _start_rows(tm, copies(slot, p1_ref, p2_ref))

    @pl.when(i + 1 < pl.num_programs(0))
    def _():
        _start_rows(tm, copies(1 - slot, p1_next_ref, p2_next_ref))

    route = route_ref[...]
    lane = lax.broadcasted_iota(jnp.int32, route.shape, 1)
    w1 = _lane_pick(route, lane, R_W1)
    w2 = _lane_pick(route, lane, R_W2)
    _wait_rows(tm, copies(slot, p1_ref, p2_ref))
    y1, y2 = y1_ref.at[slot], y2_ref.at[slot]
    f = jnp.concatenate([w1 * _rows_from_tiles(y1, c, tm) + w2 * _rows_from_tiles(y2, c, tm)
                         for c in range(SUBLANES)], axis=1)
    out_ref[...] = _layer_norm(alpha * x_ref[...] + f, g_ref[...], b_ref[...])


def _combine(x, route, y_tiles, p1, p2, g, b, *, alpha, tm):
    T, D = x.shape
    n_tiles = T // tm
    cur, nxt = _index_blocks(tm, n_tiles, 0), _index_blocks(tm, n_tiles, 1)
    slots = pltpu.VMEM((2, tm * SUBLANES, LANES), F32)
    return pl.pallas_call(
        functools.partial(_combine_kernel, alpha=alpha),
        grid=(n_tiles,),
        in_specs=[cur, cur, nxt, nxt, pl.BlockSpec((tm, D), lambda i: (i, 0)),
                  pl.BlockSpec((tm, LANES), lambda i: (i, 0)), pl.BlockSpec(memory_space=pl.ANY),
                  pl.BlockSpec((1, D), lambda i: (0, 0)), pl.BlockSpec((1, D), lambda i: (0, 0))],
        out_specs=pl.BlockSpec((tm, D), lambda i: (i, 0)),
        out_shape=jax.ShapeDtypeStruct((T, D), F32),
        scratch_shapes=[slots, slots, pltpu.SemaphoreType.DMA((2,))],
        compiler_params=pltpu.CompilerParams(dimension_semantics=("arbitrary",)),
        name="moe_combine",
    )(p1, p2, p1, p2, x, route, y_tiles, g, b)


def _tile_plan(counts, n_tiles, tm):
    padded = ((counts + tm - 1) // tm) * tm
    ends = jnp.cumsum(padded)
    offsets = ends - padded
    tile_start = jnp.arange(n_tiles, dtype=jnp.int32) * tm
    tile_expert = jnp.minimum(jnp.sum(tile_start[:, None] >= ends[None, :], axis=1), N_EXPERTS - 1).astype(jnp.int32)
    n_used = (ends[-1] // tm).astype(jnp.int32).reshape(1)
    tile_expert = jnp.where(jnp.arange(n_tiles) < n_used[0], tile_expert, tile_expert[jnp.maximum(n_used[0] - 1, 0)])
    return offsets, tile_expert, n_used


def _moe(x, x_tiles, router_w, wg, wu, wd, g, b, *, alpha, tm):
    T, D = x.shape
    assert D == SUBLANES * LANES
    r = router_w.astype(F32)
    r_hi = r.astype(BF16)
    r_lo = (r - r_hi.astype(F32)).astype(BF16)
    route, sel = _router(x, _pad_cols(r_hi, LANES), _pad_cols(r_lo, LANES), tm=tm)
    rank, counts = _rank(sel, tm=tm)
    n_tiles = (2 * T) // tm + N_EXPERTS
    offsets, tile_expert, n_used = _tile_plan(counts[0, :N_EXPERTS].astype(jnp.int32), n_tiles, tm)
    pos = _positions(route, rank, _pad_cols(offsets.astype(F32).reshape(1, -1), LANES), tm=tm)
    p1 = pos[:, 0].reshape(T // tm, 1, tm)
    p2 = pos[:, 1].reshape(T // tm, 1, tm)
    buf = _dispatch(x_tiles, p1, p2, n_tiles * tm, tm=tm)
    y_tiles = _experts(buf, tile_expert, n_used, wg, wu, wd, tm=tm)
    return _combine(x, route, y_tiles, p1, p2, g, b, alpha=alpha, tm=tm)


def _pad_cols(w, width):
    return jnp.pad(w, ((0, 0), (0, width - w.shape[1])))


def _swap_halves(w, head_dim):
    d = w.shape[1]
    idx = np.arange(d).reshape(d // head_dim, 2, head_dim // 2)[:, ::-1, :].reshape(d)
    return w[:, idx]


def _prep_in_weights(w_in, w_uq, w_ukv):
    sizes = (MLA_Q_RANK, MLA_KV_RANK, MLA_ROPE, W_RET, W_RET, W_RET, W_RET, W_MOBA, W_MOBA, W_MOBA)
    offs = np.cumsum((0,) + sizes)
    c_q, c_kv, k_rope, rq, rk, rv, rg, mq, mk, mv = (w_in[:, offs[n]:offs[n + 1]] for n in range(len(sizes)))
    w1 = jnp.concatenate([c_q, c_kv, _pad_cols(k_rope, LANES), _pad_cols(_swap_halves(k_rope, MLA_ROPE), LANES)], 1)
    w2 = jnp.concatenate([_pad_cols(w, RET_PAD) for w in
                          (rq, _swap_halves(rq, RET_DK), rk, _swap_halves(rk, RET_DK), rv, rg)], 1)
    def head_slots(w):
        w = w.reshape(w.shape[0], MOBA_HEADS, MOBA_DH)
        return jnp.pad(w, ((0, 0), (0, 0), (0, MOBA_SLOT - MOBA_DH))).reshape(w.shape[0], MOBA_SLOTS)

    w3 = jnp.concatenate([head_slots(mq), head_slots(mk), _pad_cols(mv, MOBA_PAD)], 1)

    dq = MLA_NOPE + MLA_ROPE
    uq = w_uq.reshape(MLA_Q_RANK, MLA_HEADS, dq)
    uq_rope_sw = _swap_halves(uq[:, :, MLA_NOPE:].reshape(MLA_Q_RANK, -1), MLA_ROPE).reshape(MLA_Q_RANK, MLA_HEADS, MLA_ROPE)
    zpad = jnp.zeros((MLA_Q_RANK, MLA_HEADS, MLA_SLOT - dq), w_uq.dtype)
    wuq = jnp.concatenate([uq, zpad], 2).reshape(MLA_Q_RANK, -1)
    wuq_sw = jnp.concatenate([jnp.zeros_like(uq[:, :, :MLA_NOPE]), uq_rope_sw, zpad], 2).reshape(MLA_Q_RANK, -1)
    ukv = w_ukv.reshape(MLA_KV_RANK, MLA_HEADS, MLA_NOPE + MLA_V)
    k_part = jnp.concatenate([ukv[:, :, :MLA_NOPE], jnp.zeros((MLA_KV_RANK, MLA_HEADS, MLA_SLOT - MLA_NOPE), w_ukv.dtype)], 2)
    wukv = jnp.concatenate([k_part.reshape(MLA_KV_RANK, -1), ukv[:, :, MLA_NOPE:].reshape(MLA_KV_RANK, -1)], 1)
    return tuple(w.astype(BF16) for w in (w1, w2, w3)), tuple(w.astype(BF16) for w in (wuq, wuq_sw, wukv))


def _rope_key_placement():
    p = np.zeros((LANES, MLA_HEADS * MLA_SLOT), np.float32)
    for h in range(MLA_HEADS):
        for j in range(MLA_ROPE):
            p[j, h * MLA_SLOT + MLA_NOPE + j] = 1.0
    return jnp.asarray(p, BF16)


def _rope_tables(seq):
    def cos_sin(dim):
        pos = jnp.arange(seq, dtype=F32)
        inv = ROPE_THETA ** (-jnp.arange(0, dim, 2, dtype=F32) / dim)
        ang = pos[:, None] * inv[None, :]
        c, s = jnp.cos(ang), jnp.sin(ang)
        return jnp.concatenate([c, c], 1), jnp.concatenate([-s, s], 1)

    ca, sa = cos_sin(MLA_ROPE)
    cb, sb = cos_sin(RET_DK)
    scale = (MLA_NOPE + MLA_ROPE) ** -0.5 * LOG2E
    ones = jnp.ones((seq, MLA_NOPE), F32)
    zeros_n = jnp.zeros((seq, MLA_NOPE), F32)
    zpad = jnp.zeros((seq, MLA_SLOT - MLA_NOPE - MLA_ROPE), F32)
    cq = jnp.tile(jnp.concatenate([ones, ca, zpad], 1), (1, MLA_HEADS)) * scale
    sq = jnp.tile(jnp.concatenate([zeros_n, sa, zpad], 1), (1, MLA_HEADS)) * scale
    ck = _pad_cols(ca, LANES)
    sk = _pad_cols(sa, LANES)
    cr = _pad_cols(jnp.tile(cb, (1, RET_HEADS)), RET_PAD)
    sr = _pad_cols(jnp.tile(sb, (1, RET_HEADS)), RET_PAD)
    return cq, sq, ck, sk, cr, sr


def _column_blocks(w, tf):
    *lead, D, F = w.shape
    w = w.astype(BF16).reshape(*lead, D, F // tf, tf)
    return jnp.swapaxes(w, -3, -2)


def _ff_tile(width, preferred):
    return preferred if width % preferred == 0 else width


def kernel(x, ln_emb_g, ln_emb_b, w_in, q_norm_g, kv_norm_g, w_uq, w_ukv, beta_mla, beta_moba, w_o, ln1_g, ln1_b,
           ffn_w_gate, ffn_w_up, ffn_w_down, router, exp_w_gate, exp_w_up, exp_w_down, ln2_g, ln2_b):
    B, S, D = x.shape
    depth = w_in.shape[0]
    alpha = float((2 * depth) ** 0.25)
    assert S % MOBA_BLOCK == 0 and S % RET_CHUNK == 0
    tm = 512 if S % 512 == 0 else MOBA_BLOCK
    row = lambda v: v.reshape(1, -1).astype(F32)

    tables = _rope_tables(S)
    pk = _rope_key_placement()
    alibi_lanes = _moba_alibi_lanes(tm)
    xs = x.reshape(B * S, D)
    for l in range(depth):
        w123, wmla = _prep_in_weights(w_in[l], w_uq[l], w_ukv[l])
        weights = (*w123, row(q_norm_g[l]), row(kv_norm_g[l]), *wmla, pk, *alibi_lanes)
        ln = (row(ln_emb_g), row(ln_emb_b)) if l == 0 else None
        outs = _in_proj(xs, ln, weights, tables, seq=S, tm=tm)
        if l == 0:
            xs, *outs = outs
        qm, km, vm, rq, rk, rv, rg, mq, mk, mv, kmean = outs
        o_a = _mla_attention(qm, km, vm, batch=B, seq=S, tq=MOBA_BLOCK)
        o_b = _retention(rq, rk, rv, rg, batch=B, seq=S)
        o_c = _moba_attention(mq, mk, mv, kmean, batch=B, seq=S)
        wo = w_o[l].astype(BF16)
        routed = l % 2 == 1
        xs, *x_tiles = _out_proj(o_a, o_b, o_c, xs, row(beta_mla[l]), row(beta_moba[l]),
                                 wo[:W_MLA_OUT], wo[W_MLA_OUT:W_MLA_OUT + W_RET], wo[W_MLA_OUT + W_RET:],
                                 row(ln1_g[l]), row(ln1_b[l]), alpha=alpha, tm=tm, with_row_tiles=routed)
        j = l // 2
        if routed:
            tf = _ff_tile(exp_w_gate.shape[-1], 896)
            xs = _moe(xs, *x_tiles, router[j], _column_blocks(exp_w_gate[j], tf), _column_blocks(exp_w_up[j], tf),
                      exp_w_down[j].astype(BF16), row(ln2_g[l]), row(ln2_b[l]), alpha=alpha, tm=tm)
        else:
            tf = _ff_tile(ffn_w_gate.shape[-1], 1408)
            xs = _ffn(xs, _column_blocks(ffn_w_gate[j], tf), _column_blocks(ffn_w_up[j], tf),
                      ffn_w_down[j].astype(BF16), row(ln2_g[l]), row(ln2_b[l]), alpha=alpha, tm=tm)
    return xs.reshape(B, S, D)
```

```python
import functools

import numpy as np
import jax
import jax.numpy as jnp
from jax import lax
from jax.experimental import pallas as pl
from jax.experimental.pallas import tpu as pltpu

F32 = jnp.float32
BF16 = jnp.bfloat16

MLA_HEADS, MLA_NOPE, MLA_ROPE, MLA_V = 6, 64, 32, 64
MLA_Q_RANK, MLA_KV_RANK = 256, 128
RET_HEADS, RET_DK, RET_DV, RET_CHUNK = 5, 64, 64, 128
MOBA_HEADS, MOBA_DH, MOBA_BLOCK, MOBA_TOPK = 5, 64, 256, 3
ROPE_THETA = 10000.0
LN_EPS = 1e-5
RMS_EPS = 1e-6
N_EXPERTS = 8
W_MLA_OUT = MLA_HEADS * MLA_V
W_RET = RET_HEADS * RET_DK
W_MOBA = MOBA_HEADS * MOBA_DH

LANES = 128
MLA_SLOT = 128
NEG = -1e30


def _pad_to(n, m):
    return -(-n // m) * m


RET_PAD = _pad_to(W_RET, LANES)
MOBA_PAD = _pad_to(W_MOBA, LANES)
MOBA_SLOT = 128
MOBA_SLOTS = MOBA_HEADS * MOBA_SLOT
LOG2E = 1.4426950408889634


def _layer_norm(x, g, b):
    mu = jnp.mean(x, -1, keepdims=True)
    xc = x - mu
    var = jnp.mean(xc * xc, -1, keepdims=True)
    return xc * lax.rsqrt(var + LN_EPS) * g + b


def _rms_norm(x, g):
    return x * lax.rsqrt(jnp.mean(x * x, -1, keepdims=True) + RMS_EPS) * g


def _silu(x):
    return x / (1.0 + jnp.exp(-x))


def _dot(a, b):
    return jnp.dot(a, b, preferred_element_type=F32)


def _dot_nt(a, b):
    return lax.dot_general(a, b, (((1,), (1,)), ((), ())), preferred_element_type=F32)


def _swap_rope_halves(x, rope_dim, period, offset):
    half = rope_dim // 2
    lane = lax.broadcasted_iota(jnp.int32, x.shape, 1) % period
    first_half = (lane >= offset) & (lane < offset + half)
    width = x.shape[1]
    return jnp.where(first_half, pltpu.roll(x, width - half, 1), pltpu.roll(x, half, 1))


def _in_proj_kernel(*refs, apply_ln, blocks_per_tile):
    if apply_ln:
        x_ref, lng_ref, lnb_ref, *refs = refs
    else:
        x_ref, *refs = refs
    (w1_ref, w2_ref, w3_ref, qg_ref, kvg_ref, wuq_ref, wukv_ref, pk_ref, mqa_ref, mka_ref,
     cq_ref, sq_ref, ck_ref, sk_ref, cr_ref, sr_ref, *outs) = refs
    if apply_ln:
        x0_ref, *outs = outs
    (qm_ref, km_ref, vm_ref, rq_ref, rk_ref, rv_ref, rg_ref, mq_ref, mk_ref, mv_ref, kmean_ref) = outs

    x = x_ref[...]
    if apply_ln:
        x = _layer_norm(x, lng_ref[...], lnb_ref[...])
        x0_ref[...] = x
    xb = x.astype(BF16)

    ha = _dot(xb, w1_ref[...])
    qn = _rms_norm(ha[:, :MLA_Q_RANK], qg_ref[...]).astype(BF16)
    q = _dot(qn, wuq_ref[...])
    q = q * cq_ref[...] + _swap_rope_halves(q, MLA_ROPE, MLA_SLOT, MLA_NOPE) * sq_ref[...]
    qm_ref[...] = q.astype(BF16)
    kvn = _rms_norm(ha[:, MLA_Q_RANK:MLA_Q_RANK + MLA_KV_RANK], kvg_ref[...]).astype(BF16)
    kv = _dot(kvn, wukv_ref[...])
    o = MLA_Q_RANK + MLA_KV_RANK
    kr = ha[:, o:o + LANES]
    kpe = kr * ck_ref[...] + _swap_rope_halves(kr, MLA_ROPE, LANES, 0) * sk_ref[...]
    k_width = MLA_HEADS * MLA_SLOT
    km_ref[...] = (kv[:, :k_width] + _dot(kpe.astype(BF16), pk_ref[...])).astype(BF16)
    vm_ref[...] = kv[:, k_width:].T.astype(BF16)

    hb = _dot(xb, w2_ref[...])
    P = RET_PAD
    rq, rk = hb[:, 0:P], hb[:, P:2 * P]
    rq = rq * cr_ref[...] + _swap_rope_halves(rq, RET_DK, RET_DK, 0) * sr_ref[...]
    rk = (rk * cr_ref[...] + _swap_rope_halves(rk, RET_DK, RET_DK, 0) * sr_ref[...]) * (RET_DK ** -0.5)
    rq_ref[...] = rq[:, :W_RET].astype(BF16)
    rk_ref[...] = rk[:, :W_RET].astype(BF16)
    rv_ref[...] = hb[:, 2 * P:3 * P].T.astype(BF16)
    rg_ref[...] = hb[:, 3 * P:3 * P + W_RET]

    hc = _dot(xb, w3_ref[...])
    P = MOBA_SLOTS
    mq_ref[...] = (hc[:, :P] * (MOBA_DH ** -0.5 * LOG2E) + mqa_ref[...]).astype(BF16)
    mk = hc[:, P:2 * P]
    mk_ref[...] = (mk + mka_ref[...]).astype(BF16)
    mv_ref[...] = hc[:, 2 * P:].T.astype(BF16)
    for bi in range(blocks_per_tile):
        kmean_ref[bi] = jnp.mean(mk[bi * MOBA_BLOCK:(bi + 1) * MOBA_BLOCK], axis=0, keepdims=True)


def _in_proj(x, ln, weights, tables, *, seq, tm):
    T, D = x.shape
    n_tiles = T // tm
    tiles_per_seq = seq // tm
    bpt = tm // MOBA_BLOCK
    apply_ln = ln is not None
    row = lambda i: (i, 0)
    const = lambda i: (0, 0)
    pos = lambda i: (i % tiles_per_seq, 0)

    in_specs, args = [pl.BlockSpec((tm, D), row)], [x]
    if apply_ln:
        in_specs += [pl.BlockSpec((1, D), const)] * 2
        args += list(ln)
    for w in weights:
        in_specs.append(pl.BlockSpec(w.shape, const))
        args.append(w)
    for t in tables:
        in_specs.append(pl.BlockSpec((tm, t.shape[1]), pos))
        args.append(t)

    def out(width, dtype):
        return jax.ShapeDtypeStruct((T, width), dtype), pl.BlockSpec((tm, width), row)

    outs = []
    if apply_ln:
        outs.append(out(D, F32))
    def out_t(height, dtype):
        return jax.ShapeDtypeStruct((height, T), dtype), pl.BlockSpec((height, tm), lambda i: (0, i))

    outs += [out(MLA_HEADS * MLA_SLOT, BF16), out(MLA_HEADS * MLA_SLOT, BF16), out_t(W_MLA_OUT, BF16),
             out(W_RET, BF16), out(W_RET, BF16), out_t(RET_PAD, BF16), out(W_RET, F32),
             out(MOBA_SLOTS, BF16), out(MOBA_SLOTS, BF16), out_t(MOBA_PAD, BF16)]
    outs.append((jax.ShapeDtypeStruct((T // MOBA_BLOCK, 1, MOBA_SLOTS), F32),
                 pl.BlockSpec((bpt, 1, MOBA_SLOTS), lambda i: (i, 0, 0))))
    return pl.pallas_call(
        functools.partial(_in_proj_kernel, apply_ln=apply_ln, blocks_per_tile=bpt),
        grid=(n_tiles,),
        in_specs=in_specs,
        out_specs=[o[1] for o in outs],
        out_shape=[o[0] for o in outs],
        compiler_params=pltpu.CompilerParams(dimension_semantics=("parallel",)),
        name="in_proj_ln" if apply_ln else "in_proj",
    )(*args)


SUM_ROWS = 16


def _softmax_step(s, vt, m, acc, query_bias=None):
    smax = jnp.max(s, 0, keepdims=True)
    if query_bias is not None:
        smax = smax + query_bias
    m_new = jnp.maximum(m, smax)
    shift = m_new if query_bias is None else m_new - query_bias
    p = jnp.exp2(s - shift).astype(BF16)
    vt_ones = jnp.concatenate([vt, jnp.ones((SUM_ROWS, vt.shape[1]), BF16)], axis=0)
    acc = jnp.exp2(m - m_new) * acc + _dot(vt_ones, p)
    return m_new, acc


def _softmax_init(n_heads, dv, tq):
    return tuple((jnp.full((1, tq), NEG, F32), jnp.zeros((dv + SUM_ROWS, tq), F32)) for _ in range(n_heads))


def _softmax_finish(carry, dv):
    return [acc[:dv] / acc[dv:dv + 1] for _, acc in carry]


def _attend_blocks(i, n_heads, init, score, update):
    def run(blocks, carry):
        pending = [score(blocks[0][0], h) for h in range(n_heads)]
        for n, (j, diagonal) in enumerate(blocks):
            following, new = [], []
            for h in range(n_heads):
                new.append(update(j, h, pending[h], carry[h], diagonal))
                if n + 1 < len(blocks):
                    following.append(score(blocks[n + 1][0], h))
            carry, pending = tuple(new), following
        return carry

    carry = run([(i, True)], init)
    carry = lax.cond(i % 2 == 1, lambda c: run([(i - 1, False)], c), lambda c: c, carry)
    return lax.fori_loop(0, i // 2, lambda t, c: run([(2 * t, False), (2 * t + 1, False)], c), carry)


def _causal_mask(tq):
    keys = lax.broadcasted_iota(jnp.int32, (tq, tq), 0)
    queries = lax.broadcasted_iota(jnp.int32, (tq, tq), 1)
    return keys <= queries


def _mla_attn_kernel(q_ref, k_ref, vt_ref, o_ref, *, tq):
    i = pl.program_id(1)
    causal = _causal_mask(tq)
    slots = [slice(MLA_SLOT * h, MLA_SLOT * (h + 1)) for h in range(MLA_HEADS)]

    def keys(j):
        return pl.ds(pl.multiple_of(j * tq, tq), tq)

    def score(j, h):
        return _dot_nt(k_ref[keys(j), slots[h]], q_ref[:, slots[h]])

    def update(j, h, s, state, diagonal):
        s = jnp.where(causal, s, NEG) if diagonal else s
        return _softmax_step(s, vt_ref[MLA_V * h:MLA_V * (h + 1), keys(j)], *state)

    carry = _attend_blocks(i, MLA_HEADS, _softmax_init(MLA_HEADS, MLA_V, tq), score, update)
    out = jnp.concatenate(_softmax_finish(carry, MLA_V), axis=0)
    o_ref[...] = out.T


def _mla_attention(q, k, vt, *, batch, seq, tq):
    T = q.shape[0]
    nq = seq // tq
    return pl.pallas_call(
        functools.partial(_mla_attn_kernel, tq=tq),
        grid=(batch, nq),
        in_specs=[pl.BlockSpec((tq, q.shape[1]), lambda b, i: (b * nq + i, 0)),
                  pl.BlockSpec((seq, k.shape[1]), lambda b, i: (b, 0)),
                  pl.BlockSpec((vt.shape[0], seq), lambda b, i: (0, b))],
        out_specs=pl.BlockSpec((tq, W_MLA_OUT), lambda b, i: (b * nq + i, 0)),
        out_shape=jax.ShapeDtypeStruct((T, W_MLA_OUT), F32),
        compiler_params=pltpu.CompilerParams(dimension_semantics=("parallel", "arbitrary")),
        name="mla_attention",
    )(q, k, vt)


SLOPE_PARTS = 3


def _moba_slopes_log2():
    return tuple(float(np.sum(_slope_parts(h), dtype=np.float32)) for h in range(MOBA_HEADS))


def _slope_parts(h):
    rest = np.float32(2.0 ** (-8.0 * (h + 1.0) / MOBA_HEADS) * LOG2E)
    parts = []
    for _ in range(SLOPE_PARTS):
        piece = np.float32(rest.astype(BF16))
        parts.append(piece)
        rest = np.float32(rest - piece)
    return np.asarray(parts, np.float32)


def _moba_alibi_lanes(rows):
    q_add = np.zeros((1, MOBA_SLOTS), np.float32)
    k_add = np.zeros((rows, MOBA_SLOTS), np.float32)
    offset = (np.arange(rows) % MOBA_BLOCK).astype(np.float32)
    for h in range(MOBA_HEADS):
        lanes = slice(h * MOBA_SLOT + MOBA_DH, h * MOBA_SLOT + MOBA_DH + SLOPE_PARTS)
        q_add[0, lanes] = _slope_parts(h)
        k_add[:, lanes] = offset[:, None]
    return jnp.asarray(q_add), jnp.asarray(k_add)


def _moba_kernel(q_ref, k_ref, vt_ref, kmean_ref, o_ref, bias_ref, *, n_blocks, slopes):
    tq = MOBA_BLOCK
    i = pl.program_id(1)
    causal = _causal_mask(tq)
    blk_id = lax.broadcasted_iota(jnp.int32, (n_blocks, tq), 0)
    past = blk_id < i

    slots = [slice(MOBA_SLOT * h, MOBA_SLOT * (h + 1)) for h in range(MOBA_HEADS)]
    for h in range(MOBA_HEADS):
        q = q_ref[:, slots[h]]
        km = kmean_ref[:, 0, slots[h]]
        km_hi = km.astype(BF16)
        km_lo = (km - km_hi.astype(F32)).astype(BF16)
        gate = _dot_nt(km_hi, q) + _dot_nt(km_lo, q)
        gate = jnp.where(past, gate, -jnp.inf)
        rank = jnp.zeros((n_blocks, tq), F32)
        for mb in range(n_blocks):
            gm = gate[mb:mb + 1, :]
            beats = (gm > gate) | ((gm == gate) & (blk_id > mb))
            rank = rank + jnp.where(beats, 1.0, 0.0)
        chosen = past & (rank < MOBA_TOPK)
        bias = jnp.where(chosen, 0.0, NEG)
        for nb in range(n_blocks):
            bias_ref[h, nb] = bias[nb:nb + 1, :]

    def keys(j):
        return pl.ds(pl.multiple_of(j * tq, tq), tq)

    def score(j, h):
        return _dot_nt(k_ref[keys(j), slots[h]], q_ref[:, slots[h]])

    def update(j, h, s, state, diagonal):
        vt = vt_ref[MOBA_DH * h:MOBA_DH * (h + 1), keys(j)]
        if diagonal:
            return _softmax_step(jnp.where(causal, s, NEG), vt, *state)
        query_bias = bias_ref[h, j] + slopes[h] * ((j - i) * tq).astype(F32)
        return _softmax_step(s, vt, *state, query_bias=query_bias)

    carry = _attend_blocks(i, MOBA_HEADS, _softmax_init(MOBA_HEADS, MOBA_DH, tq), score, update)
    pad = jnp.zeros((MOBA_PAD - W_MOBA, tq), F32)
    out = jnp.concatenate(_softmax_finish(carry, MOBA_DH) + [pad], axis=0)
    o_ref[...] = out.T[:, :W_MOBA]


def _moba_attention(q, k, vt, kmean, *, batch, seq):
    T = q.shape[0]
    tq = MOBA_BLOCK
    nq = seq // tq
    slopes = _moba_slopes_log2()
    return pl.pallas_call(
        functools.partial(_moba_kernel, n_blocks=nq, slopes=slopes),
        grid=(batch, nq),
        in_specs=[pl.BlockSpec((tq, MOBA_SLOTS), lambda b, i: (b * nq + i, 0)),
                  pl.BlockSpec((seq, MOBA_SLOTS), lambda b, i: (b, 0)),
                  pl.BlockSpec((vt.shape[0], seq), lambda b, i: (0, b)),
                  pl.BlockSpec((nq, 1, MOBA_SLOTS), lambda b, i: (b, 0, 0))],
        out_specs=pl.BlockSpec((tq, W_MOBA), lambda b, i: (b * nq + i, 0)),
        out_shape=jax.ShapeDtypeStruct((T, W_MOBA), F32),
        scratch_shapes=[pltpu.VMEM((MOBA_HEADS, nq, 1, tq), F32)],
        compiler_params=pltpu.CompilerParams(dimension_semantics=("parallel", "arbitrary")),
        name="moba_attention",
    )(q, k, vt, kmean)


RET_TILE = 256


def _retention_kernel(q_ref, k_ref, vt_ref, g_ref, din_ref, qdec_ref, kdec_ref, o_ref, state_ref, *, chunk_decay):
    @pl.when(pl.program_id(1) == 0)
    def _():
        state_ref[...] = jnp.zeros_like(state_ref)

    C = q_ref.shape[0]
    heads = [slice(RET_DK * h, RET_DK * (h + 1)) for h in range(RET_HEADS)]
    kd = (k_ref[...].astype(F32) * kdec_ref[...]).astype(BF16)
    scores = [_dot_nt(k_ref[:, hs], q_ref[:, hs]) for hs in heads]
    outs = []
    for h, hs in enumerate(heads):
        vt = vt_ref[hs, :]
        state = state_ref[h]
        o = _dot(vt, (scores[h] * din_ref[h]).astype(BF16))
        o = o + _dot_nt(state.astype(BF16), q_ref[:, hs]) * qdec_ref[h]
        state_ref[h] = state * chunk_decay[h] + _dot(vt, kd[:, hs])
        oc = o - jnp.mean(o, 0, keepdims=True)
        outs.append(oc * lax.rsqrt(jnp.mean(oc * oc, 0, keepdims=True) + RMS_EPS))
    outs.append(jnp.zeros((RET_PAD - W_RET, C), F32))
    normed = jnp.concatenate(outs, axis=0).T[:, :W_RET]
    o_ref[...] = (_silu(g_ref[...]) * normed).astype(BF16)


def _retention(q, k, vt, g, *, batch, seq):
    T = q.shape[0]
    C = RET_TILE
    nc = seq // C
    log_gamma = np.log(1.0 - 2.0 ** (-5.0 - np.arange(RET_HEADS, dtype=np.float64)))
    idx = np.arange(C, dtype=np.float64)
    lag = idx[None, :] - idx[:, None]
    decay_in = np.where(lag >= 0, np.exp(log_gamma[:, None, None] * np.maximum(lag, 0.0)), 0.0)
    q_decay = np.exp(log_gamma[:, None, None] * (idx[None, None, :] + 1.0))
    k_decay = np.repeat(np.exp(log_gamma[None, :] * (C - 1.0 - idx[:, None])), RET_DK, axis=1)
    chunk_decay = tuple(float(c) for c in np.exp(log_gamma * C))
    tile = lambda b, c: (b * nc + c, 0)
    const3 = lambda b, c: (0, 0, 0)
    return pl.pallas_call(
        functools.partial(_retention_kernel, chunk_decay=chunk_decay),
        grid=(batch, nc),
        in_specs=[pl.BlockSpec((C, W_RET), tile), pl.BlockSpec((C, W_RET), tile),
                  pl.BlockSpec((vt.shape[0], C), lambda b, c: (0, b * nc + c)), pl.BlockSpec((C, W_RET), tile),
                  pl.BlockSpec((RET_HEADS, C, C), const3), pl.BlockSpec((RET_HEADS, 1, C), const3),
                  pl.BlockSpec((C, W_RET), lambda b, c: (0, 0))],
        out_specs=pl.BlockSpec((C, W_RET), tile),
        out_shape=jax.ShapeDtypeStruct((T, W_RET), BF16),
        scratch_shapes=[pltpu.VMEM((RET_HEADS, RET_DV, RET_DK), F32)],
        compiler_params=pltpu.CompilerParams(dimension_semantics=("parallel", "arbitrary")),
        name="retention",
    )(q, k, vt, g, jnp.asarray(decay_in, F32), jnp.asarray(q_decay, F32), jnp.asarray(k_decay, F32))


def _out_proj_kernel(oa_ref, ob_ref, oc_ref, x_ref, ba_ref, bc_ref, wo_ref, g_ref, b_ref,
                     y_ref, *maybe_tiles_ref, alpha):
    na = _rms_norm(oa_ref[...], ba_ref[...]).astype(BF16)
    nc = _rms_norm(oc_ref[...], bc_ref[...]).astype(BF16)
    mix = _dot(jnp.concatenate([na, ob_ref[...], nc], axis=1), wo_ref[...])
    y = _layer_norm(alpha * x_ref[...] + mix, g_ref[...], b_ref[...])
    y_ref[...] = y
    for tiles_ref in maybe_tiles_ref:
        for c in range(SUBLANES):
            tiles_ref[pl.ds(c, y.shape[0], stride=SUBLANES), :] = y[:, c * LANES:(c + 1) * LANES]


def _out_proj(oa, ob, oc, x, beta_a, beta_c, wo, g, b, *, alpha, tm, with_row_tiles):
    T, D = x.shape
    row = lambda i: (i, 0)
    const = lambda i: (0, 0)
    full = lambda a: pl.BlockSpec(a.shape, const)
    out_specs = [pl.BlockSpec((tm, D), row)]
    out_shape = [jax.ShapeDtypeStruct((T, D), F32)]
    if with_row_tiles:
        assert D == SUBLANES * LANES
        out_specs.append(pl.BlockSpec((tm * SUBLANES, LANES), row))
        out_shape.append(jax.ShapeDtypeStruct((T * SUBLANES, LANES), F32))
    return pl.pallas_call(
        functools.partial(_out_proj_kernel, alpha=alpha),
        grid=(T // tm,),
        in_specs=[pl.BlockSpec((tm, oa.shape[1]), row), pl.BlockSpec((tm, ob.shape[1]), row),
                  pl.BlockSpec((tm, oc.shape[1]), row), pl.BlockSpec((tm, D), row),
                  full(beta_a), full(beta_c), full(wo), full(g), full(b)],
        out_specs=out_specs,
        out_shape=out_shape,
        compiler_params=pltpu.CompilerParams(dimension_semantics=("parallel",)),
        name="out_proj_tiles" if with_row_tiles else "out_proj",
    )(oa, ob, oc, x, beta_a, beta_c, wo, g, b)


def _ffn_kernel(x_ref, wg_ref, wu_ref, wd_ref, g_ref, b_ref, y_ref, xb_ref, acc_ref, *, alpha):
    j = pl.program_id(1)

    @pl.when(j == 0)
    def _():
        xb_ref[...] = x_ref[...].astype(BF16)
        acc_ref[...] = jnp.zeros_like(acc_ref)

    xb = xb_ref[...]
    h = _silu(_dot(xb, wg_ref[...])) * _dot(xb, wu_ref[...])
    acc_ref[...] += _dot(h.astype(BF16), wd_ref[...])

    @pl.when(j == pl.num_programs(1) - 1)
    def _():
        y_ref[...] = _layer_norm(alpha * x_ref[...] + acc_ref[...], g_ref[...], b_ref[...])


def _ffn(x, wg, wu, wd, g, b, *, alpha, tm, tf):
    T, D = x.shape
    F = wg.shape[1]
    return pl.pallas_call(
        functools.partial(_ffn_kernel, alpha=alpha),
        grid=(T // tm, F // tf),
        in_specs=[pl.BlockSpec((tm, D), lambda i, j: (i, 0)),
                  pl.BlockSpec((D, tf), lambda i, j: (0, j)),
                  pl.BlockSpec((D, tf), lambda i, j: (0, j)),
                  pl.BlockSpec((tf, D), lambda i, j: (j, 0)),
                  pl.BlockSpec((1, D), lambda i, j: (0, 0)),
                  pl.BlockSpec((1, D), lambda i, j: (0, 0))],
        out_specs=pl.BlockSpec((tm, D), lambda i, j: (i, 0)),
        out_shape=jax.ShapeDtypeStruct((T, D), F32),
        scratch_shapes=[pltpu.VMEM((tm, D), BF16), pltpu.VMEM((tm, D), F32)],
        compiler_params=pltpu.CompilerParams(dimension_semantics=("parallel", "arbitrary")),
        name="ffn",
    )(x, wg, wu, wd, g, b)


SUBLANES = 8
R_E1, R_E2, R_W1, R_W2 = 0, 1, 2, 3


def _lane_pick(x, lane, idx):
    return jnp.sum(jnp.where(lane == idx, x, 0.0), -1, keepdims=True)


def _router_kernel(x_ref, rhi_ref, rlo_ref, route_ref, sel_ref):
    x = x_ref[...]
    x_hi = x.astype(BF16)
    x_lo = (x - x_hi.astype(F32)).astype(BF16)
    logits = _dot(x_hi, rhi_ref[...]) + _dot(x_lo, rhi_ref[...]) + _dot(x_hi, rlo_ref[...])
    lane = lax.broadcasted_iota(jnp.int32, logits.shape, 1)
    logits = jnp.where(lane < N_EXPERTS, logits, -jnp.inf)
    m1 = jnp.max(logits, -1, keepdims=True)
    i1 = jnp.min(jnp.where(logits == m1, lane, LANES), -1, keepdims=True)
    rest = jnp.where(lane == i1, -jnp.inf, logits)
    m2 = jnp.max(rest, -1, keepdims=True)
    i2 = jnp.min(jnp.where(rest == m2, lane, LANES), -1, keepdims=True)
    e2 = jnp.exp(m2 - m1)
    w1 = 1.0 / (1.0 + e2)
    w2 = e2 / (1.0 + e2)
    route = jnp.where(lane == R_E1, i1.astype(F32), 0.0) + jnp.where(lane == R_E2, i2.astype(F32), 0.0)
    route_ref[...] = route + jnp.where(lane == R_W1, w1, 0.0) + jnp.where(lane == R_W2, w2, 0.0)
    sel_ref[...] = jnp.where((lane == i1) | (lane == i2), 1.0, 0.0).astype(BF16)


def _router(x, r_hi, r_lo, *, tm):
    T, D = x.shape
    return pl.pallas_call(
        _router_kernel,
        grid=(T // tm,),
        in_specs=[pl.BlockSpec((tm, D), lambda i: (i, 0)),
                  pl.BlockSpec((D, LANES), lambda i: (0, 0)),
                  pl.BlockSpec((D, LANES), lambda i: (0, 0))],
        out_specs=[pl.BlockSpec((tm, LANES), lambda i: (i, 0)), pl.BlockSpec((tm, LANES), lambda i: (i, 0))],
        out_shape=[jax.ShapeDtypeStruct((T, LANES), F32), jax.ShapeDtypeStruct((T, LANES), BF16)],
        compiler_params=pltpu.CompilerParams(dimension_semantics=("parallel",)),
        name="router",
    )(x, r_hi, r_lo)


def _rank_kernel(sel_ref, rank_ref, count_ref, carry_ref):
    @pl.when(pl.program_id(0) == 0)
    def _():
        carry_ref[...] = jnp.zeros_like(carry_ref)

    sel = sel_ref[...]
    tm = sel.shape[0]
    earlier = lax.broadcasted_iota(jnp.int32, (tm, tm), 1) < lax.broadcasted_iota(jnp.int32, (tm, tm), 0)
    rank_ref[...] = _dot(jnp.where(earlier, 1.0, 0.0).astype(BF16), sel) + carry_ref[...]
    carry_ref[...] += jnp.sum(sel.astype(F32), 0, keepdims=True)
    count_ref[...] = carry_ref[...]


def _rank(sel, *, tm):
    T = sel.shape[0]
    return pl.pallas_call(
        _rank_kernel,
        grid=(T // tm,),
        in_specs=[pl.BlockSpec((tm, LANES), lambda i: (i, 0))],
        out_specs=[pl.BlockSpec((tm, LANES), lambda i: (i, 0)), pl.BlockSpec((1, LANES), lambda i: (0, 0))],
        out_shape=[jax.ShapeDtypeStruct((T, LANES), F32), jax.ShapeDtypeStruct((1, LANES), F32)],
        scratch_shapes=[pltpu.VMEM((1, LANES), F32)],
        compiler_params=pltpu.CompilerParams(dimension_semantics=("arbitrary",)),
        name="moe_rank",
    )(sel)


def _position_kernel(route_ref, rank_ref, off_ref, pos_ref):
    route = route_ref[...]
    lane = lax.broadcasted_iota(jnp.int32, route.shape, 1)
    e1 = _lane_pick(route, lane, R_E1).astype(jnp.int32)
    e2 = _lane_pick(route, lane, R_E2).astype(jnp.int32)
    slot = rank_ref[...] + off_ref[...]
    p1 = _lane_pick(slot, lane, e1)
    p2 = _lane_pick(slot, lane, e2)
    pos_ref[...] = (jnp.where(lane == 0, p1, 0.0) + jnp.where(lane == 1, p2, 0.0)).astype(jnp.int32)


def _positions(route, rank, offsets, *, tm):
    T = route.shape[0]
    return pl.pallas_call(
        _position_kernel,
        grid=(T // tm,),
        in_specs=[pl.BlockSpec((tm, LANES), lambda i: (i, 0)), pl.BlockSpec((tm, LANES), lambda i: (i, 0)),
                  pl.BlockSpec((1, LANES), lambda i: (0, 0))],
        out_specs=pl.BlockSpec((tm, LANES), lambda i: (i, 0)),
        out_shape=jax.ShapeDtypeStruct((T, LANES), jnp.int32),
        compiler_params=pltpu.CompilerParams(dimension_semantics=("parallel",)),
        name="moe_positions",
    )(route, rank, offsets)


def _row_copy(src_ref, src_row, dst_ref, dst_row, sem):
    return pltpu.make_async_copy(
        src_ref.at[pl.ds(pl.multiple_of(src_row * SUBLANES, SUBLANES), SUBLANES), :],
        dst_ref.at[pl.ds(pl.multiple_of(dst_row * SUBLANES, SUBLANES), SUBLANES), :],
        sem)


def _start_then_wait(n_rows, copies):
    def start(r, c):
        for cp in copies(r):
            cp.start()
        return c

    def wait(r, c):
        for cp in copies(r):
            cp.wait()
        return c

    lax.fori_loop(0, n_rows, start, 0)
    return lambda: lax.fori_loop(0, n_rows, wait, 0)


def _dispatch_kernel(p1_ref, p2_ref, x_ref, buf_in_ref, buf_ref, sem):
    del buf_in_ref
    wait_all = _start_then_wait(p1_ref.shape[-1], lambda r: (
        _row_copy(x_ref, r, buf_ref, p1_ref[0, r], sem), _row_copy(x_ref, r, buf_ref, p2_ref[0, r], sem)))
    wait_all()


def _dispatch(x_tiles, p1, p2, n_rows, *, tm):
    T = x_tiles.shape[0] // SUBLANES
    buf = jnp.zeros((n_rows * SUBLANES, LANES), F32)
    smem = lambda: pl.BlockSpec((None, 1, tm), lambda i: (i, 0, 0), memory_space=pltpu.SMEM)
    return pl.pallas_call(
        _dispatch_kernel,
        grid=(T // tm,),
        in_specs=[smem(), smem(), pl.BlockSpec((tm * SUBLANES, LANES), lambda i: (i, 0)),
                  pl.BlockSpec(memory_space=pl.ANY)],
        out_specs=pl.BlockSpec(memory_space=pl.ANY),
        out_shape=jax.ShapeDtypeStruct(buf.shape, F32),
        input_output_aliases={3: 0},
        scratch_shapes=[pltpu.SemaphoreType.DMA(())],
        compiler_params=pltpu.CompilerParams(dimension_semantics=("arbitrary",)),
        name="moe_dispatch",
    )(p1, p2, x_tiles, buf)


def _rows_from_tiles(ref, c, rows):
    return ref[pl.ds(c, rows, stride=SUBLANES), :]


def _expert_kernel(tile_expert_ref, n_used_ref, x_ref, wg_ref, wu_ref, wd_ref, y_ref, xb_ref, acc_ref):
    del tile_expert_ref
    g = pl.program_id(0)
    j = pl.program_id(1)
    tm = xb_ref.shape[0]
    used = g < n_used_ref[0]

    @pl.when(used & (j == 0))
    def _():
        for c in range(SUBLANES):
            xb_ref[:, c * LANES:(c + 1) * LANES] = _rows_from_tiles(x_ref, c, tm).astype(BF16)
        acc_ref[...] = jnp.zeros_like(acc_ref)

    @pl.when(used)
    def _():
        xb = xb_ref[...]
        h = _silu(_dot(xb, wg_ref[...])) * _dot(xb, wu_ref[...])
        acc_ref[...] += _dot(h.astype(BF16), wd_ref[...])

    @pl.when(j == pl.num_programs(1) - 1)
    def _():
        for c in range(SUBLANES):
            cols = acc_ref[:, c * LANES:(c + 1) * LANES]
            y_ref[pl.ds(c, tm, stride=SUBLANES), :] = jnp.where(used, cols, 0.0)


def _experts(buf, tile_expert, n_used, wg, wu, wd, *, tm, tf):
    n_tiles = buf.shape[0] // (tm * SUBLANES)
    E, D, F = wg.shape
    x_map = lambda g, j, te, nu: (g, 0)
    grid_spec = pltpu.PrefetchScalarGridSpec(
        num_scalar_prefetch=2,
        grid=(n_tiles, F // tf),
        in_specs=[pl.BlockSpec((tm * SUBLANES, LANES), x_map),
                  pl.BlockSpec((None, D, tf), lambda g, j, te, nu: (te[g], 0, j)),
                  pl.BlockSpec((None, D, tf), lambda g, j, te, nu: (te[g], 0, j)),
                  pl.BlockSpec((None, tf, D), lambda g, j, te, nu: (te[g], j, 0))],
        out_specs=pl.BlockSpec((tm * SUBLANES, LANES), x_map),
        scratch_shapes=[pltpu.VMEM((tm, D), BF16), pltpu.VMEM((tm, D), F32)],
    )
    return pl.pallas_call(
        _expert_kernel,
        grid_spec=grid_spec,
        out_shape=jax.ShapeDtypeStruct(buf.shape, F32),
        compiler_params=pltpu.CompilerParams(dimension_semantics=("parallel", "arbitrary")),
        name="moe_experts",
    )(tile_expert, n_used, buf, wg, wu, wd)


def _combine_kernel(p1_ref, p2_ref, x_ref, route_ref, y_hbm_ref, g_ref, b_ref, out_ref, y1_ref, y2_ref, sem, *, alpha):
    tm = x_ref.shape[0]
    wait_all = _start_then_wait(tm, lambda r: (
        _row_copy(y_hbm_ref, p1_ref[0, r], y1_ref, r, sem), _row_copy(y_hbm_ref, p2_ref[0, r], y2_ref, r, sem)))
    route = route_ref[...]
    lane = lax.broadcasted_iota(jnp.int32, route.shape, 1)
    w1 = _lane_pick(route, lane, R_W1)
    w2 = _lane_pick(route, lane, R_W2)
    wait_all()
    f = jnp.concatenate([w1 * _rows_from_tiles(y1_ref, c, tm) + w2 * _rows_from_tiles(y2_ref, c, tm)
                         for c in range(SUBLANES)], axis=1)
    out_ref[...] = _layer_norm(alpha * x_ref[...] + f, g_ref[...], b_ref[...])


def _combine(x, route, y_tiles, p1, p2, g, b, *, alpha, tm):
    T, D = x.shape
    smem = lambda: pl.BlockSpec((None, 1, tm), lambda i: (i, 0, 0), memory_space=pltpu.SMEM)
    return pl.pallas_call(
        functools.partial(_combine_kernel, alpha=alpha),
        grid=(T // tm,),
        in_specs=[smem(), smem(), pl.BlockSpec((tm, D), lambda i: (i, 0)), pl.BlockSpec((tm, LANES), lambda i: (i, 0)),
                  pl.BlockSpec(memory_space=pl.ANY),
                  pl.BlockSpec((1, D), lambda i: (0, 0)), pl.BlockSpec((1, D), lambda i: (0, 0))],
        out_specs=pl.BlockSpec((tm, D), lambda i: (i, 0)),
        out_shape=jax.ShapeDtypeStruct((T, D), F32),
        scratch_shapes=[pltpu.VMEM((tm * SUBLANES, LANES), F32), pltpu.VMEM((tm * SUBLANES, LANES), F32),
                        pltpu.SemaphoreType.DMA(())],
        compiler_params=pltpu.CompilerParams(dimension_semantics=("arbitrary",)),
        name="moe_combine",
    )(p1, p2, x, route, y_tiles, g, b)


def _tile_plan(counts, n_tiles, tm):
    padded = ((counts + tm - 1) // tm) * tm
    ends = jnp.cumsum(padded)
    offsets = ends - padded
    tile_start = jnp.arange(n_tiles, dtype=jnp.int32) * tm
    tile_expert = jnp.minimum(jnp.sum(tile_start[:, None] >= ends[None, :], axis=1), N_EXPERTS - 1).astype(jnp.int32)
    n_used = (ends[-1] // tm).astype(jnp.int32).reshape(1)
    tile_expert = jnp.where(jnp.arange(n_tiles) < n_used[0], tile_expert, tile_expert[jnp.maximum(n_used[0] - 1, 0)])
    return offsets, tile_expert, n_used


def _moe(x, x_tiles, router_w, wg, wu, wd, g, b, *, alpha, tm, tf):
    T, D = x.shape
    assert D == SUBLANES * LANES
    r = router_w.astype(F32)
    r_hi = r.astype(BF16)
    r_lo = (r - r_hi.astype(F32)).astype(BF16)
    route, sel = _router(x, _pad_cols(r_hi, LANES), _pad_cols(r_lo, LANES), tm=tm)
    rank, counts = _rank(sel, tm=tm)
    n_tiles = (2 * T) // tm + N_EXPERTS
    offsets, tile_expert, n_used = _tile_plan(counts[0, :N_EXPERTS].astype(jnp.int32), n_tiles, tm)
    pos = _positions(route, rank, _pad_cols(offsets.astype(F32).reshape(1, -1), LANES), tm=tm)
    p1 = pos[:, 0].reshape(T // tm, 1, tm)
    p2 = pos[:, 1].reshape(T // tm, 1, tm)
    buf = _dispatch(x_tiles, p1, p2, n_tiles * tm, tm=tm)
    y_tiles = _experts(buf, tile_expert, n_used, wg, wu, wd, tm=tm, tf=tf)
    return _combine(x, route, y_tiles, p1, p2, g, b, alpha=alpha, tm=tm)


def _pad_cols(w, width):
    return jnp.pad(w, ((0, 0), (0, width - w.shape[1])))


def _prep_in_weights(w_in, w_uq, w_ukv):
    sizes = (MLA_Q_RANK, MLA_KV_RANK, MLA_ROPE, W_RET, W_RET, W_RET, W_RET, W_MOBA, W_MOBA, W_MOBA)
    offs = np.cumsum((0,) + sizes)
    c_q, c_kv, k_rope, rq, rk, rv, rg, mq, mk, mv = (w_in[:, offs[n]:offs[n + 1]] for n in range(len(sizes)))
    w1 = jnp.concatenate([c_q, c_kv, _pad_cols(k_rope, LANES)], 1)
    w2 = jnp.concatenate([_pad_cols(w, RET_PAD) for w in (rq, rk, rv, rg)], 1)
    def head_slots(w):
        w = w.reshape(w.shape[0], MOBA_HEADS, MOBA_DH)
        return jnp.pad(w, ((0, 0), (0, 0), (0, MOBA_SLOT - MOBA_DH))).reshape(w.shape[0], MOBA_SLOTS)

    w3 = jnp.concatenate([head_slots(mq), head_slots(mk), _pad_cols(mv, MOBA_PAD)], 1)

    dq = MLA_NOPE + MLA_ROPE
    uq = w_uq.reshape(MLA_Q_RANK, MLA_HEADS, dq)
    zpad = jnp.zeros((MLA_Q_RANK, MLA_HEADS, MLA_SLOT - dq), w_uq.dtype)
    wuq = jnp.concatenate([uq, zpad], 2).reshape(MLA_Q_RANK, -1)
    ukv = w_ukv.reshape(MLA_KV_RANK, MLA_HEADS, MLA_NOPE + MLA_V)
    k_part = jnp.concatenate([ukv[:, :, :MLA_NOPE], jnp.zeros((MLA_KV_RANK, MLA_HEADS, MLA_SLOT - MLA_NOPE), w_ukv.dtype)], 2)
    wukv = jnp.concatenate([k_part.reshape(MLA_KV_RANK, -1), ukv[:, :, MLA_NOPE:].reshape(MLA_KV_RANK, -1)], 1)
    return tuple(w.astype(BF16) for w in (w1, w2, w3)), tuple(w.astype(BF16) for w in (wuq, wukv))


def _rope_key_placement():
    p = np.zeros((LANES, MLA_HEADS * MLA_SLOT), np.float32)
    for h in range(MLA_HEADS):
        for j in range(MLA_ROPE):
            p[j, h * MLA_SLOT + MLA_NOPE + j] = 1.0
    return jnp.asarray(p, BF16)


def _rope_tables(seq):
    def cos_sin(dim):
        pos = jnp.arange(seq, dtype=F32)
        inv = ROPE_THETA ** (-jnp.arange(0, dim, 2, dtype=F32) / dim)
        ang = pos[:, None] * inv[None, :]
        c, s = jnp.cos(ang), jnp.sin(ang)
        return jnp.concatenate([c, c], 1), jnp.concatenate([-s, s], 1)

    ca, sa = cos_sin(MLA_ROPE)
    cb, sb = cos_sin(RET_DK)
    scale = (MLA_NOPE + MLA_ROPE) ** -0.5 * LOG2E
    ones = jnp.ones((seq, MLA_NOPE), F32)
    zeros_n = jnp.zeros((seq, MLA_NOPE), F32)
    zpad = jnp.zeros((seq, MLA_SLOT - MLA_NOPE - MLA_ROPE), F32)
    cq = jnp.tile(jnp.concatenate([ones, ca, zpad], 1), (1, MLA_HEADS)) * scale
    sq = jnp.tile(jnp.concatenate([zeros_n, sa, zpad], 1), (1, MLA_HEADS)) * scale
    ck = _pad_cols(ca, LANES)
    sk = _pad_cols(sa, LANES)
    cr = _pad_cols(jnp.tile(cb, (1, RET_HEADS)), RET_PAD)
    sr = _pad_cols(jnp.tile(sb, (1, RET_HEADS)), RET_PAD)
    return cq, sq, ck, sk, cr, sr


def _ff_tile(width, preferred):
    return preferred if width % preferred == 0 else width


def kernel(x, ln_emb_g, ln_emb_b, w_in, q_norm_g, kv_norm_g, w_uq, w_ukv, beta_mla, beta_moba, w_o, ln1_g, ln1_b,
           ffn_w_gate, ffn_w_up, ffn_w_down, router, exp_w_gate, exp_w_up, exp_w_down, ln2_g, ln2_b):
    B, S, D = x.shape
    depth = w_in.shape[0]
    alpha = float((2 * depth) ** 0.25)
    assert S % MOBA_BLOCK == 0 and S % RET_CHUNK == 0
    tm = 512 if S % 512 == 0 else MOBA_BLOCK
    row = lambda v: v.reshape(1, -1).astype(F32)

    tables = _rope_tables(S)
    pk = _rope_key_placement()
    alibi_lanes = _moba_alibi_lanes(tm)
    xs = x.reshape(B * S, D)
    for l in range(depth):
        w123, wmla = _prep_in_weights(w_in[l], w_uq[l], w_ukv[l])
        weights = (*w123, row(q_norm_g[l]), row(kv_norm_g[l]), *wmla, pk, *alibi_lanes)
        ln = (row(ln_emb_g), row(ln_emb_b)) if l == 0 else None
        outs = _in_proj(xs, ln, weights, tables, seq=S, tm=tm)
        if l == 0:
            xs, *outs = outs
        qm, km, vm, rq, rk, rv, rg, mq, mk, mv, kmean = outs
        o_a = _mla_attention(qm, km, vm, batch=B, seq=S, tq=MOBA_BLOCK)
        o_b = _retention(rq, rk, rv, rg, batch=B, seq=S)
        o_c = _moba_attention(mq, mk, mv, kmean, batch=B, seq=S)
        wo = w_o[l].astype(BF16)
        routed = l % 2 == 1
        xs, *x_tiles = _out_proj(o_a, o_b, o_c, xs, row(beta_mla[l]), row(beta_moba[l]), wo,
                                 row(ln1_g[l]), row(ln1_b[l]), alpha=alpha, tm=tm, with_row_tiles=routed)
        j = l // 2
        if routed:
            xs = _moe(xs, *x_tiles, router[j], exp_w_gate[j].astype(BF16), exp_w_up[j].astype(BF16),
                      exp_w_down[j].astype(BF16), row(ln2_g[l]), row(ln2_b[l]), alpha=alpha, tm=tm,
                      tf=_ff_tile(exp_w_gate.shape[-1], 1792))
        else:
            xs = _ffn(xs, ffn_w_gate[j].astype(BF16), ffn_w_up[j].astype(BF16), ffn_w_down[j].astype(BF16),
                      row(ln2_g[l]), row(ln2_b[l]), alpha=alpha, tm=tm, tf=_ff_tile(ffn_w_gate.shape[-1], 1408))
    return xs.reshape(B, S, D)
```

```python
import functools

import numpy as np
import jax
import jax.numpy as jnp
from jax import lax
from jax.experimental import pallas as pl
from jax.experimental.pallas import tpu as pltpu

F32 = jnp.float32
BF16 = jnp.bfloat16

MLA_HEADS, MLA_NOPE, MLA_ROPE, MLA_V = 6, 64, 32, 64
MLA_Q_RANK, MLA_KV_RANK = 256, 128
RET_HEADS, RET_DK, RET_DV, RET_CHUNK = 5, 64, 64, 128
MOBA_HEADS, MOBA_DH, MOBA_BLOCK, MOBA_TOPK = 5, 64, 256, 3
ROPE_THETA = 10000.0
LN_EPS = 1e-5
RMS_EPS = 1e-6
N_EXPERTS = 8
W_MLA_OUT = MLA_HEADS * MLA_V
W_RET = RET_HEADS * RET_DK
W_MOBA = MOBA_HEADS * MOBA_DH

LANES = 128
MLA_SLOT = 128
NEG = -1e30


def _pad_to(n, m):
    return -(-n // m) * m


RET_PAD = _pad_to(W_RET, LANES)
MOBA_PAD = _pad_to(W_MOBA, LANES)
MOBA_SLOT = 128
MOBA_SLOTS = MOBA_HEADS * MOBA_SLOT
LOG2E = 1.4426950408889634


def _layer_norm(x, g, b):
    mu = jnp.mean(x, -1, keepdims=True)
    xc = x - mu
    var = jnp.mean(xc * xc, -1, keepdims=True)
    return xc * lax.rsqrt(var + LN_EPS) * g + b


def _rms_norm(x, g):
    return x * lax.rsqrt(jnp.mean(x * x, -1, keepdims=True) + RMS_EPS) * g


def _silu(x):
    return x / (1.0 + jnp.exp(-x))


def _dot(a, b):
    return jnp.dot(a, b, preferred_element_type=F32)


def _dot_nt(a, b):
    return lax.dot_general(a, b, (((1,), (1,)), ((), ())), preferred_element_type=F32)


def _swap_rope_halves(x, rope_dim, period, offset):
    half = rope_dim // 2
    lane = lax.broadcasted_iota(jnp.int32, x.shape, 1) % period
    first_half = (lane >= offset) & (lane < offset + half)
    width = x.shape[1]
    return jnp.where(first_half, pltpu.roll(x, width - half, 1), pltpu.roll(x, half, 1))


def _in_proj_kernel(*refs, apply_ln, blocks_per_tile):
    if apply_ln:
        x_ref, lng_ref, lnb_ref, *refs = refs
    else:
        x_ref, *refs = refs
    (w1_ref, w2_ref, w3_ref, qg_ref, kvg_ref, wuq_ref, wukv_ref, pk_ref, mqa_ref, mka_ref,
     cq_ref, sq_ref, ck_ref, sk_ref, cr_ref, sr_ref, *outs) = refs
    if apply_ln:
        x0_ref, *outs = outs
    (qm_ref, km_ref, vm_ref, rq_ref, rk_ref, rv_ref, rg_ref, mq_ref, mk_ref, mv_ref, kmean_ref) = outs

    x = x_ref[...]
    if apply_ln:
        x = _layer_norm(x, lng_ref[...], lnb_ref[...])
        x0_ref[...] = x
    xb = x.astype(BF16)

    ha = _dot(xb, w1_ref[...])
    qn = _rms_norm(ha[:, :MLA_Q_RANK], qg_ref[...]).astype(BF16)
    q = _dot(qn, wuq_ref[...])
    q = q * cq_ref[...] + _swap_rope_halves(q, MLA_ROPE, MLA_SLOT, MLA_NOPE) * sq_ref[...]
    qm_ref[...] = q.astype(BF16)
    kvn = _rms_norm(ha[:, MLA_Q_RANK:MLA_Q_RANK + MLA_KV_RANK], kvg_ref[...]).astype(BF16)
    kv = _dot(kvn, wukv_ref[...])
    o = MLA_Q_RANK + MLA_KV_RANK
    kr = ha[:, o:o + LANES]
    kpe = kr * ck_ref[...] + _swap_rope_halves(kr, MLA_ROPE, LANES, 0) * sk_ref[...]
    k_width = MLA_HEADS * MLA_SLOT
    km_ref[...] = (kv[:, :k_width] + _dot(kpe.astype(BF16), pk_ref[...])).astype(BF16)
    vm_ref[...] = kv[:, k_width:].T.astype(BF16)

    hb = _dot(xb, w2_ref[...])
    P = RET_PAD
    rq, rk = hb[:, 0:P], hb[:, P:2 * P]
    rq = rq * cr_ref[...] + _swap_rope_halves(rq, RET_DK, RET_DK, 0) * sr_ref[...]
    rk = (rk * cr_ref[...] + _swap_rope_halves(rk, RET_DK, RET_DK, 0) * sr_ref[...]) * (RET_DK ** -0.5)
    rq_ref[...] = rq[:, :W_RET].astype(BF16)
    rk_ref[...] = rk[:, :W_RET].astype(BF16)
    rv_ref[...] = hb[:, 2 * P:3 * P].T.astype(BF16)
    rg_ref[...] = hb[:, 3 * P:3 * P + W_RET]

    hc = _dot(xb, w3_ref[...])
    P = MOBA_SLOTS
    mq_ref[...] = (hc[:, :P] * (MOBA_DH ** -0.5 * LOG2E) + mqa_ref[...]).astype(BF16)
    mk = hc[:, P:2 * P]
    mk_ref[...] = (mk + mka_ref[...]).astype(BF16)
    mv_ref[...] = hc[:, 2 * P:].T.astype(BF16)
    for bi in range(blocks_per_tile):
        kmean_ref[bi] = jnp.mean(mk[bi * MOBA_BLOCK:(bi + 1) * MOBA_BLOCK], axis=0, keepdims=True)


def _in_proj(x, ln, weights, tables, *, seq, tm):
    T, D = x.shape
    n_tiles = T // tm
    tiles_per_seq = seq // tm
    bpt = tm // MOBA_BLOCK
    apply_ln = ln is not None
    row = lambda i: (i, 0)
    const = lambda i: (0, 0)
    pos = lambda i: (i % tiles_per_seq, 0)

    in_specs, args = [pl.BlockSpec((tm, D), row)], [x]
    if apply_ln:
        in_specs += [pl.BlockSpec((1, D), const)] * 2
        args += list(ln)
    for w in weights:
        in_specs.append(pl.BlockSpec(w.shape, const))
        args.append(w)
    for t in tables:
        in_specs.append(pl.BlockSpec((tm, t.shape[1]), pos))
        args.append(t)

    def out(width, dtype):
        return jax.ShapeDtypeStruct((T, width), dtype), pl.BlockSpec((tm, width), row)

    outs = []
    if apply_ln:
        outs.append(out(D, F32))
    def out_t(height, dtype):
        return jax.ShapeDtypeStruct((height, T), dtype), pl.BlockSpec((height, tm), lambda i: (0, i))

    outs += [out(MLA_HEADS * MLA_SLOT, BF16), out(MLA_HEADS * MLA_SLOT, BF16), out_t(W_MLA_OUT, BF16),
             out(W_RET, BF16), out(W_RET, BF16), out_t(RET_PAD, BF16), out(W_RET, F32),
             out(MOBA_SLOTS, BF16), out(MOBA_SLOTS, BF16), out_t(MOBA_PAD, BF16)]
    outs.append((jax.ShapeDtypeStruct((T // MOBA_BLOCK, 1, MOBA_SLOTS), F32),
                 pl.BlockSpec((bpt, 1, MOBA_SLOTS), lambda i: (i, 0, 0))))
    return pl.pallas_call(
        functools.partial(_in_proj_kernel, apply_ln=apply_ln, blocks_per_tile=bpt),
        grid=(n_tiles,),
        in_specs=in_specs,
        out_specs=[o[1] for o in outs],
        out_shape=[o[0] for o in outs],
        compiler_params=pltpu.CompilerParams(dimension_semantics=("parallel",)),
        name="in_proj_ln" if apply_ln else "in_proj",
    )(*args)


SUM_ROWS = 16


def _softmax_step(s, vt, m, acc, query_bias=None):
    smax = jnp.max(s, 0, keepdims=True)
    if query_bias is not None:
        smax = smax + query_bias
    m_new = jnp.maximum(m, smax)
    shift = m_new if query_bias is None else m_new - query_bias
    p = jnp.exp2(s - shift).astype(BF16)
    vt_ones = jnp.concatenate([vt, jnp.ones((SUM_ROWS, vt.shape[1]), BF16)], axis=0)
    acc = jnp.exp2(m - m_new) * acc + _dot(vt_ones, p)
    return m_new, acc


def _softmax_init(n_heads, dv, tq):
    return tuple((jnp.full((1, tq), NEG, F32), jnp.zeros((dv + SUM_ROWS, tq), F32)) for _ in range(n_heads))


def _softmax_finish(carry, dv):
    return [acc[:dv] / acc[dv:dv + 1] for _, acc in carry]


def _attend_blocks(i, n_heads, init, score, update):
    def run(blocks, carry):
        pending = [score(blocks[0][0], h) for h in range(n_heads)]
        for n, (j, diagonal) in enumerate(blocks):
            following, new = [], []
            for h in range(n_heads):
                new.append(update(j, h, pending[h], carry[h], diagonal))
                if n + 1 < len(blocks):
                    following.append(score(blocks[n + 1][0], h))
            carry, pending = tuple(new), following
        return carry

    def past(first, count):
        return [(first + n, False) for n in range(count)]

    carry = run([(i, True)], init)
    carry = lax.cond(i % 2 == 1, lambda c: run(past(i - 1, 1), c), lambda c: c, carry)
    pairs = i // 2
    carry = lax.cond(pairs % 2 == 1, lambda c: run(past(2 * (pairs - 1), 2), c), lambda c: c, carry)
    return lax.fori_loop(0, pairs // 2, lambda t, c: run(past(4 * t, 4), c), carry)


def _causal_mask(tq):
    keys = lax.broadcasted_iota(jnp.int32, (tq, tq), 0)
    queries = lax.broadcasted_iota(jnp.int32, (tq, tq), 1)
    return keys <= queries


def _mla_attn_kernel(q_ref, k_ref, vt_ref, o_ref, *, tq):
    i = pl.program_id(1)
    causal = _causal_mask(tq)
    slots = [slice(MLA_SLOT * h, MLA_SLOT * (h + 1)) for h in range(MLA_HEADS)]

    def keys(j):
        return pl.ds(pl.multiple_of(j * tq, tq), tq)

    def score(j, h):
        return _dot_nt(k_ref[keys(j), slots[h]], q_ref[:, slots[h]])

    def update(j, h, s, state, diagonal):
        s = jnp.where(causal, s, NEG) if diagonal else s
        return _softmax_step(s, vt_ref[MLA_V * h:MLA_V * (h + 1), keys(j)], *state)

    carry = _attend_blocks(i, MLA_HEADS, _softmax_init(MLA_HEADS, MLA_V, tq), score, update)
    out = jnp.concatenate(_softmax_finish(carry, MLA_V), axis=0)
    o_ref[...] = out.T


def _mla_attention(q, k, vt, *, batch, seq, tq):
    T = q.shape[0]
    nq = seq // tq
    return pl.pallas_call(
        functools.partial(_mla_attn_kernel, tq=tq),
        grid=(batch, nq),
        in_specs=[pl.BlockSpec((tq, q.shape[1]), lambda b, i: (b * nq + i, 0)),
                  pl.BlockSpec((seq, k.shape[1]), lambda b, i: (b, 0)),
                  pl.BlockSpec((vt.shape[0], seq), lambda b, i: (0, b))],
        out_specs=pl.BlockSpec((tq, W_MLA_OUT), lambda b, i: (b * nq + i, 0)),
        out_shape=jax.ShapeDtypeStruct((T, W_MLA_OUT), F32),
        compiler_params=pltpu.CompilerParams(dimension_semantics=("parallel", "arbitrary")),
        name="mla_attention",
    )(q, k, vt)


SLOPE_PARTS = 3


def _moba_slopes_log2():
    return tuple(float(np.sum(_slope_parts(h), dtype=np.float32)) for h in range(MOBA_HEADS))


def _slope_parts(h):
    rest = np.float32(2.0 ** (-8.0 * (h + 1.0) / MOBA_HEADS) * LOG2E)
    parts = []
    for _ in range(SLOPE_PARTS):
        piece = np.float32(rest.astype(BF16))
        parts.append(piece)
        rest = np.float32(rest - piece)
    return np.asarray(parts, np.float32)


def _moba_alibi_lanes(rows):
    q_add = np.zeros((1, MOBA_SLOTS), np.float32)
    k_add = np.zeros((rows, MOBA_SLOTS), np.float32)
    offset = (np.arange(rows) % MOBA_BLOCK).astype(np.float32)
    for h in range(MOBA_HEADS):
        lanes = slice(h * MOBA_SLOT + MOBA_DH, h * MOBA_SLOT + MOBA_DH + SLOPE_PARTS)
        q_add[0, lanes] = _slope_parts(h)
        k_add[:, lanes] = offset[:, None]
    return jnp.asarray(q_add), jnp.asarray(k_add)


def _moba_kernel(q_ref, k_ref, vt_ref, kmean_ref, o_ref, bias_ref, *, n_blocks, slopes):
    tq = MOBA_BLOCK
    i = pl.program_id(1)
    causal = _causal_mask(tq)
    blk_id = lax.broadcasted_iota(jnp.int32, (n_blocks, tq), 0)
    past = blk_id < i

    slots = [slice(MOBA_SLOT * h, MOBA_SLOT * (h + 1)) for h in range(MOBA_HEADS)]
    for h in range(MOBA_HEADS):
        q = q_ref[:, slots[h]]
        km = kmean_ref[:, 0, slots[h]]
        km_hi = km.astype(BF16)
        km_lo = (km - km_hi.astype(F32)).astype(BF16)
        gate = _dot_nt(km_hi, q) + _dot_nt(km_lo, q)
        gate = jnp.where(past, gate, -jnp.inf)
        rank = jnp.zeros((n_blocks, tq), F32)
        for mb in range(n_blocks):
            gm = gate[mb:mb + 1, :]
            beats = (gm > gate) | ((gm == gate) & (blk_id > mb))
            rank = rank + jnp.where(beats, 1.0, 0.0)
        chosen = past & (rank < MOBA_TOPK)
        bias = jnp.where(chosen, 0.0, NEG)
        for nb in range(n_blocks):
            bias_ref[h, nb] = bias[nb:nb + 1, :]

    def keys(j):
        return pl.ds(pl.multiple_of(j * tq, tq), tq)

    def score(j, h):
        return _dot_nt(k_ref[keys(j), slots[h]], q_ref[:, slots[h]])

    def update(j, h, s, state, diagonal):
        vt = vt_ref[MOBA_DH * h:MOBA_DH * (h + 1), keys(j)]
        if diagonal:
            return _softmax_step(jnp.where(causal, s, NEG), vt, *state)
        query_bias = bias_ref[h, j] + slopes[h] * ((j - i) * tq).astype(F32)
        return _softmax_step(s, vt, *state, query_bias=query_bias)

    carry = _attend_blocks(i, MOBA_HEADS, _softmax_init(MOBA_HEADS, MOBA_DH, tq), score, update)
    pad = jnp.zeros((MOBA_PAD - W_MOBA, tq), F32)
    out = jnp.concatenate(_softmax_finish(carry, MOBA_DH) + [pad], axis=0)
    o_ref[...] = out.T[:, :W_MOBA]


def _moba_attention(q, k, vt, kmean, *, batch, seq):
    T = q.shape[0]
    tq = MOBA_BLOCK
    nq = seq // tq
    slopes = _moba_slopes_log2()
    return pl.pallas_call(
        functools.partial(_moba_kernel, n_blocks=nq, slopes=slopes),
        grid=(batch, nq),
        in_specs=[pl.BlockSpec((tq, MOBA_SLOTS), lambda b, i: (b * nq + i, 0)),
                  pl.BlockSpec((seq, MOBA_SLOTS), lambda b, i: (b, 0)),
                  pl.BlockSpec((vt.shape[0], seq), lambda b, i: (0, b)),
                  pl.BlockSpec((nq, 1, MOBA_SLOTS), lambda b, i: (b, 0, 0))],
        out_specs=pl.BlockSpec((tq, W_MOBA), lambda b, i: (b * nq + i, 0)),
        out_shape=jax.ShapeDtypeStruct((T, W_MOBA), F32),
        scratch_shapes=[pltpu.VMEM((MOBA_HEADS, nq, 1, tq), F32)],
        compiler_params=pltpu.CompilerParams(dimension_semantics=("parallel", "arbitrary")),
        name="moba_attention",
    )(q, k, vt, kmean)


RET_TILE = 256


def _retention_kernel(q_ref, k_ref, vt_ref, g_ref, din_ref, qdec_ref, kdec_ref, o_ref, state_ref, *, chunk_decay):
    @pl.when(pl.program_id(1) == 0)
    def _():
        state_ref[...] = jnp.zeros_like(state_ref)

    C = q_ref.shape[0]
    heads = [slice(RET_DK * h, RET_DK * (h + 1)) for h in range(RET_HEADS)]
    kd = (k_ref[...].astype(F32) * kdec_ref[...]).astype(BF16)
    scores = [_dot_nt(k_ref[:, hs], q_ref[:, hs]) for hs in heads]
    outs = []
    for h, hs in enumerate(heads):
        vt = vt_ref[hs, :]
        state = state_ref[h]
        o = _dot(vt, (scores[h] * din_ref[h]).astype(BF16))
        o = o + _dot_nt(state.astype(BF16), q_ref[:, hs]) * qdec_ref[h]
        state_ref[h] = state * chunk_decay[h] + _dot(vt, kd[:, hs])
        oc = o - jnp.mean(o, 0, keepdims=True)
        outs.append(oc * lax.rsqrt(jnp.mean(oc * oc, 0, keepdims=True) + RMS_EPS))
    outs.append(jnp.zeros((RET_PAD - W_RET, C), F32))
    normed = jnp.concatenate(outs, axis=0).T[:, :W_RET]
    o_ref[...] = (_silu(g_ref[...]) * normed).astype(BF16)


def _retention(q, k, vt, g, *, batch, seq):
    T = q.shape[0]
    C = RET_TILE
    nc = seq // C
    log_gamma = np.log(1.0 - 2.0 ** (-5.0 - np.arange(RET_HEADS, dtype=np.float64)))
    idx = np.arange(C, dtype=np.float64)
    lag = idx[None, :] - idx[:, None]
    decay_in = np.where(lag >= 0, np.exp(log_gamma[:, None, None] * np.maximum(lag, 0.0)), 0.0)
    q_decay = np.exp(log_gamma[:, None, None] * (idx[None, None, :] + 1.0))
    k_decay = np.repeat(np.exp(log_gamma[None, :] * (C - 1.0 - idx[:, None])), RET_DK, axis=1)
    chunk_decay = tuple(float(c) for c in np.exp(log_gamma * C))
    tile = lambda b, c: (b * nc + c, 0)
    const3 = lambda b, c: (0, 0, 0)
    return pl.pallas_call(
        functools.partial(_retention_kernel, chunk_decay=chunk_decay),
        grid=(batch, nc),
        in_specs=[pl.BlockSpec((C, W_RET), tile), pl.BlockSpec((C, W_RET), tile),
                  pl.BlockSpec((vt.shape[0], C), lambda b, c: (0, b * nc + c)), pl.BlockSpec((C, W_RET), tile),
                  pl.BlockSpec((RET_HEADS, C, C), const3), pl.BlockSpec((RET_HEADS, 1, C), const3),
                  pl.BlockSpec((C, W_RET), lambda b, c: (0, 0))],
        out_specs=pl.BlockSpec((C, W_RET), tile),
        out_shape=jax.ShapeDtypeStruct((T, W_RET), BF16),
        scratch_shapes=[pltpu.VMEM((RET_HEADS, RET_DV, RET_DK), F32)],
        compiler_params=pltpu.CompilerParams(dimension_semantics=("parallel", "arbitrary")),
        name="retention",
    )(q, k, vt, g, jnp.asarray(decay_in, F32), jnp.asarray(q_decay, F32), jnp.asarray(k_decay, F32))


def _out_proj_kernel(oa_ref, ob_ref, oc_ref, x_ref, ba_ref, bc_ref, wo_ref, g_ref, b_ref,
                     y_ref, *maybe_tiles_ref, alpha):
    na = _rms_norm(oa_ref[...], ba_ref[...]).astype(BF16)
    nc = _rms_norm(oc_ref[...], bc_ref[...]).astype(BF16)
    mix = _dot(jnp.concatenate([na, ob_ref[...], nc], axis=1), wo_ref[...])
    y = _layer_norm(alpha * x_ref[...] + mix, g_ref[...], b_ref[...])
    y_ref[...] = y
    for tiles_ref in maybe_tiles_ref:
        for c in range(SUBLANES):
            tiles_ref[pl.ds(c, y.shape[0], stride=SUBLANES), :] = y[:, c * LANES:(c + 1) * LANES]


def _out_proj(oa, ob, oc, x, beta_a, beta_c, wo, g, b, *, alpha, tm, with_row_tiles):
    T, D = x.shape
    row = lambda i: (i, 0)
    const = lambda i: (0, 0)
    full = lambda a: pl.BlockSpec(a.shape, const)
    out_specs = [pl.BlockSpec((tm, D), row)]
    out_shape = [jax.ShapeDtypeStruct((T, D), F32)]
    if with_row_tiles:
        assert D == SUBLANES * LANES
        out_specs.append(pl.BlockSpec((tm * SUBLANES, LANES), row))
        out_shape.append(jax.ShapeDtypeStruct((T * SUBLANES, LANES), F32))
    return pl.pallas_call(
        functools.partial(_out_proj_kernel, alpha=alpha),
        grid=(T // tm,),
        in_specs=[pl.BlockSpec((tm, oa.shape[1]), row), pl.BlockSpec((tm, ob.shape[1]), row),
                  pl.BlockSpec((tm, oc.shape[1]), row), pl.BlockSpec((tm, D), row),
                  full(beta_a), full(beta_c), full(wo), full(g), full(b)],
        out_specs=out_specs,
        out_shape=out_shape,
        compiler_params=pltpu.CompilerParams(dimension_semantics=("parallel",)),
        name="out_proj_tiles" if with_row_tiles else "out_proj",
    )(oa, ob, oc, x, beta_a, beta_c, wo, g, b)


def _ffn_kernel(x_ref, wg_ref, wu_ref, wd_ref, g_ref, b_ref, y_ref, xb_ref, acc_ref, *, alpha):
    j = pl.program_id(1)

    @pl.when(j == 0)
    def _():
        xb_ref[...] = x_ref[...].astype(BF16)
        acc_ref[...] = jnp.zeros_like(acc_ref)

    xb = xb_ref[...]
    h = _silu(_dot(xb, wg_ref[...])) * _dot(xb, wu_ref[...])
    acc_ref[...] += _dot(h.astype(BF16), wd_ref[...])

    @pl.when(j == pl.num_programs(1) - 1)
    def _():
        y_ref[...] = _layer_norm(alpha * x_ref[...] + acc_ref[...], g_ref[...], b_ref[...])


def _ffn(x, wg, wu, wd, g, b, *, alpha, tm, tf):
    T, D = x.shape
    F = wg.shape[1]
    return pl.pallas_call(
        functools.partial(_ffn_kernel, alpha=alpha),
        grid=(T // tm, F // tf),
        in_specs=[pl.BlockSpec((tm, D), lambda i, j: (i, 0)),
                  pl.BlockSpec((D, tf), lambda i, j: (0, j)),
                  pl.BlockSpec((D, tf), lambda i, j: (0, j)),
                  pl.BlockSpec((tf, D), lambda i, j: (j, 0)),
                  pl.BlockSpec((1, D), lambda i, j: (0, 0)),
                  pl.BlockSpec((1, D), lambda i, j: (0, 0))],
        out_specs=pl.BlockSpec((tm, D), lambda i, j: (i, 0)),
        out_shape=jax.ShapeDtypeStruct((T, D), F32),
        scratch_shapes=[pltpu.VMEM((tm, D), BF16), pltpu.VMEM((tm, D), F32)],
        compiler_params=pltpu.CompilerParams(dimension_semantics=("parallel", "arbitrary")),
        name="ffn",
    )(x, wg, wu, wd, g, b)


SUBLANES = 8
R_E1, R_E2, R_W1, R_W2 = 0, 1, 2, 3


def _lane_pick(x, lane, idx):
    return jnp.sum(jnp.where(lane == idx, x, 0.0), -1, keepdims=True)


def _router_kernel(x_ref, rhi_ref, rlo_ref, route_ref, sel_ref):
    x = x_ref[...]
    x_hi = x.astype(BF16)
    x_lo = (x - x_hi.astype(F32)).astype(BF16)
    logits = _dot(x_hi, rhi_ref[...]) + _dot(x_lo, rhi_ref[...]) + _dot(x_hi, rlo_ref[...])
    lane = lax.broadcasted_iota(jnp.int32, logits.shape, 1)
    logits = jnp.where(lane < N_EXPERTS, logits, -jnp.inf)
    m1 = jnp.max(logits, -1, keepdims=True)
    i1 = jnp.min(jnp.where(logits == m1, lane, LANES), -1, keepdims=True)
    rest = jnp.where(lane == i1, -jnp.inf, logits)
    m2 = jnp.max(rest, -1, keepdims=True)
    i2 = jnp.min(jnp.where(rest == m2, lane, LANES), -1, keepdims=True)
    e2 = jnp.exp(m2 - m1)
    w1 = 1.0 / (1.0 + e2)
    w2 = e2 / (1.0 + e2)
    route = jnp.where(lane == R_E1, i1.astype(F32), 0.0) + jnp.where(lane == R_E2, i2.astype(F32), 0.0)
    route_ref[...] = route + jnp.where(lane == R_W1, w1, 0.0) + jnp.where(lane == R_W2, w2, 0.0)
    sel_ref[...] = jnp.where((lane == i1) | (lane == i2), 1.0, 0.0).astype(BF16)


def _router(x, r_hi, r_lo, *, tm):
    T, D = x.shape
    return pl.pallas_call(
        _router_kernel,
        grid=(T // tm,),
        in_specs=[pl.BlockSpec((tm, D), lambda i: (i, 0)),
                  pl.BlockSpec((D, LANES), lambda i: (0, 0)),
                  pl.BlockSpec((D, LANES), lambda i: (0, 0))],
        out_specs=[pl.BlockSpec((tm, LANES), lambda i: (i, 0)), pl.BlockSpec((tm, LANES), lambda i: (i, 0))],
        out_shape=[jax.ShapeDtypeStruct((T, LANES), F32), jax.ShapeDtypeStruct((T, LANES), BF16)],
        compiler_params=pltpu.CompilerParams(dimension_semantics=("parallel",)),
        name="router",
    )(x, r_hi, r_lo)


def _rank_kernel(sel_ref, rank_ref, count_ref, carry_ref):
    @pl.when(pl.program_id(0) == 0)
    def _():
        carry_ref[...] = jnp.zeros_like(carry_ref)

    sel = sel_ref[...]
    tm = sel.shape[0]
    earlier = lax.broadcasted_iota(jnp.int32, (tm, tm), 1) < lax.broadcasted_iota(jnp.int32, (tm, tm), 0)
    rank_ref[...] = _dot(jnp.where(earlier, 1.0, 0.0).astype(BF16), sel) + carry_ref[...]
    carry_ref[...] += jnp.sum(sel.astype(F32), 0, keepdims=True)
    count_ref[...] = carry_ref[...]


def _rank(sel, *, tm):
    T = sel.shape[0]
    return pl.pallas_call(
        _rank_kernel,
        grid=(T // tm,),
        in_specs=[pl.BlockSpec((tm, LANES), lambda i: (i, 0))],
        out_specs=[pl.BlockSpec((tm, LANES), lambda i: (i, 0)), pl.BlockSpec((1, LANES), lambda i: (0, 0))],
        out_shape=[jax.ShapeDtypeStruct((T, LANES), F32), jax.ShapeDtypeStruct((1, LANES), F32)],
        scratch_shapes=[pltpu.VMEM((1, LANES), F32)],
        compiler_params=pltpu.CompilerParams(dimension_semantics=("arbitrary",)),
        name="moe_rank",
    )(sel)


def _position_kernel(route_ref, rank_ref, off_ref, pos_ref):
    route = route_ref[...]
    lane = lax.broadcasted_iota(jnp.int32, route.shape, 1)
    e1 = _lane_pick(route, lane, R_E1).astype(jnp.int32)
    e2 = _lane_pick(route, lane, R_E2).astype(jnp.int32)
    slot = rank_ref[...] + off_ref[...]
    p1 = _lane_pick(slot, lane, e1)
    p2 = _lane_pick(slot, lane, e2)
    pos_ref[...] = (jnp.where(lane == 0, p1, 0.0) + jnp.where(lane == 1, p2, 0.0)).astype(jnp.int32)


def _positions(route, rank, offsets, *, tm):
    T = route.shape[0]
    return pl.pallas_call(
        _position_kernel,
        grid=(T // tm,),
        in_specs=[pl.BlockSpec((tm, LANES), lambda i: (i, 0)), pl.BlockSpec((tm, LANES), lambda i: (i, 0)),
                  pl.BlockSpec((1, LANES), lambda i: (0, 0))],
        out_specs=pl.BlockSpec((tm, LANES), lambda i: (i, 0)),
        out_shape=jax.ShapeDtypeStruct((T, LANES), jnp.int32),
        compiler_params=pltpu.CompilerParams(dimension_semantics=("parallel",)),
        name="moe_positions",
    )(route, rank, offsets)


def _row_copy(src_ref, src_row, dst_ref, dst_row, sem):
    return pltpu.make_async_copy(
        src_ref.at[pl.ds(pl.multiple_of(src_row * SUBLANES, SUBLANES), SUBLANES), :],
        dst_ref.at[pl.ds(pl.multiple_of(dst_row * SUBLANES, SUBLANES), SUBLANES), :],
        sem)


def _start_then_wait(n_rows, copies):
    def start(r, c):
        for cp in copies(r):
            cp.start()
        return c

    def wait(r, c):
        for cp in copies(r):
            cp.wait()
        return c

    lax.fori_loop(0, n_rows, start, 0)
    return lambda: lax.fori_loop(0, n_rows, wait, 0)


def _dispatch_kernel(p1_ref, p2_ref, x_ref, buf_in_ref, buf_ref, sem):
    del buf_in_ref
    wait_all = _start_then_wait(p1_ref.shape[-1], lambda r: (
        _row_copy(x_ref, r, buf_ref, p1_ref[0, r], sem), _row_copy(x_ref, r, buf_ref, p2_ref[0, r], sem)))
    wait_all()


def _dispatch(x_tiles, p1, p2, n_rows, *, tm):
    T = x_tiles.shape[0] // SUBLANES
    buf = jnp.zeros((n_rows * SUBLANES, LANES), F32)
    smem = lambda: pl.BlockSpec((None, 1, tm), lambda i: (i, 0, 0), memory_space=pltpu.SMEM)
    return pl.pallas_call(
        _dispatch_kernel,
        grid=(T // tm,),
        in_specs=[smem(), smem(), pl.BlockSpec((tm * SUBLANES, LANES), lambda i: (i, 0)),
                  pl.BlockSpec(memory_space=pl.ANY)],
        out_specs=pl.BlockSpec(memory_space=pl.ANY),
        out_shape=jax.ShapeDtypeStruct(buf.shape, F32),
        input_output_aliases={3: 0},
        scratch_shapes=[pltpu.SemaphoreType.DMA(())],
        compiler_params=pltpu.CompilerParams(dimension_semantics=("arbitrary",)),
        name="moe_dispatch",
    )(p1, p2, x_tiles, buf)


def _rows_from_tiles(ref, c, rows):
    return ref[pl.ds(c, rows, stride=SUBLANES), :]


def _expert_kernel(tile_expert_ref, n_used_ref, x_ref, wg_ref, wu_ref, wd_ref, y_ref, xb_ref, acc_ref):
    del tile_expert_ref
    g = pl.program_id(0)
    j = pl.program_id(1)
    tm = xb_ref.shape[0]
    used = g < n_used_ref[0]

    @pl.when(used & (j == 0))
    def _():
        for c in range(SUBLANES):
            xb_ref[:, c * LANES:(c + 1) * LANES] = _rows_from_tiles(x_ref, c, tm).astype(BF16)
        acc_ref[...] = jnp.zeros_like(acc_ref)

    @pl.when(used)
    def _():
        xb = xb_ref[...]
        h = _silu(_dot(xb, wg_ref[...])) * _dot(xb, wu_ref[...])
        acc_ref[...] += _dot(h.astype(BF16), wd_ref[...])

    @pl.when(j == pl.num_programs(1) - 1)
    def _():
        for c in range(SUBLANES):
            cols = acc_ref[:, c * LANES:(c + 1) * LANES]
            y_ref[pl.ds(c, tm, stride=SUBLANES), :] = jnp.where(used, cols, 0.0)


def _experts(buf, tile_expert, n_used, wg, wu, wd, *, tm, tf):
    n_tiles = buf.shape[0] // (tm * SUBLANES)
    E, D, F = wg.shape
    x_map = lambda g, j, te, nu: (g, 0)
    grid_spec = pltpu.PrefetchScalarGridSpec(
        num_scalar_prefetch=2,
        grid=(n_tiles, F // tf),
        in_specs=[pl.BlockSpec((tm * SUBLANES, LANES), x_map),
                  pl.BlockSpec((None, D, tf), lambda g, j, te, nu: (te[g], 0, j)),
                  pl.BlockSpec((None, D, tf), lambda g, j, te, nu: (te[g], 0, j)),
                  pl.BlockSpec((None, tf, D), lambda g, j, te, nu: (te[g], j, 0))],
        out_specs=pl.BlockSpec((tm * SUBLANES, LANES), x_map),
        scratch_shapes=[pltpu.VMEM((tm, D), BF16), pltpu.VMEM((tm, D), F32)],
    )
    return pl.pallas_call(
        _expert_kernel,
        grid_spec=grid_spec,
        out_shape=jax.ShapeDtypeStruct(buf.shape, F32),
        compiler_params=pltpu.CompilerParams(dimension_semantics=("parallel", "arbitrary")),
        name="moe_experts",
    )(tile_expert, n_used, buf, wg, wu, wd)


def _combine_kernel(p1_ref, p2_ref, x_ref, route_ref, y_hbm_ref, g_ref, b_ref, out_ref, y1_ref, y2_ref, sem, *, alpha):
    tm = x_ref.shape[0]
    wait_all = _start_then_wait(tm, lambda r: (
        _row_copy(y_hbm_ref, p1_ref[0, r], y1_ref, r, sem), _row_copy(y_hbm_ref, p2_ref[0, r], y2_ref, r, sem)))
    route = route_ref[...]
    lane = lax.broadcasted_iota(jnp.int32, route.shape, 1)
    w1 = _lane_pick(route, lane, R_W1)
    w2 = _lane_pick(route, lane, R_W2)
    wait_all()
    f = jnp.concatenate([w1 * _rows_from_tiles(y1_ref, c, tm) + w2 * _rows_from_tiles(y2_ref, c, tm)
                         for c in range(SUBLANES)], axis=1)
    out_ref[...] = _layer_norm(alpha * x_ref[...] + f, g_ref[...], b_ref[...])


def _combine(x, route, y_tiles, p1, p2, g, b, *, alpha, tm):
    T, D = x.shape
    smem = lambda: pl.BlockSpec((None, 1, tm), lambda i: (i, 0, 0), memory_space=pltpu.SMEM)
    return pl.pallas_call(
        functools.partial(_combine_kernel, alpha=alpha),
        grid=(T // tm,),
        in_specs=[smem(), smem(), pl.BlockSpec((tm, D), lambda i: (i, 0)), pl.BlockSpec((tm, LANES), lambda i: (i, 0)),
                  pl.BlockSpec(memory_space=pl.ANY),
                  pl.BlockSpec((1, D), lambda i: (0, 0)), pl.BlockSpec((1, D), lambda i: (0, 0))],
        out_specs=pl.BlockSpec((tm, D), lambda i: (i, 0)),
        out_shape=jax.ShapeDtypeStruct((T, D), F32),
        scratch_shapes=[pltpu.VMEM((tm * SUBLANES, LANES), F32), pltpu.VMEM((tm * SUBLANES, LANES), F32),
                        pltpu.SemaphoreType.DMA(())],
        compiler_params=pltpu.CompilerParams(dimension_semantics=("arbitrary",)),
        name="moe_combine",
    )(p1, p2, x, route, y_tiles, g, b)


def _tile_plan(counts, n_tiles, tm):
    padded = ((counts + tm - 1) // tm) * tm
    ends = jnp.cumsum(padded)
    offsets = ends - padded
    tile_start = jnp.arange(n_tiles, dtype=jnp.int32) * tm
    tile_expert = jnp.minimum(jnp.sum(tile_start[:, None] >= ends[None, :], axis=1), N_EXPERTS - 1).astype(jnp.int32)
    n_used = (ends[-1] // tm).astype(jnp.int32).reshape(1)
    tile_expert = jnp.where(jnp.arange(n_tiles) < n_used[0], tile_expert, tile_expert[jnp.maximum(n_used[0] - 1, 0)])
    return offsets, tile_expert, n_used


def _moe(x, x_tiles, router_w, wg, wu, wd, g, b, *, alpha, tm, te, tf):
    T, D = x.shape
    assert D == SUBLANES * LANES
    r = router_w.astype(F32)
    r_hi = r.astype(BF16)
    r_lo = (r - r_hi.astype(F32)).astype(BF16)
    route, sel = _router(x, _pad_cols(r_hi, LANES), _pad_cols(r_lo, LANES), tm=tm)
    rank, counts = _rank(sel, tm=tm)
    n_tiles = -(-2 * T // te) + N_EXPERTS
    offsets, tile_expert, n_used = _tile_plan(counts[0, :N_EXPERTS].astype(jnp.int32), n_tiles, te)
    pos = _positions(route, rank, _pad_cols(offsets.astype(F32).reshape(1, -1), LANES), tm=tm)
    p1 = pos[:, 0].reshape(T // tm, 1, tm)
    p2 = pos[:, 1].reshape(T // tm, 1, tm)
    buf = _dispatch(x_tiles, p1, p2, n_tiles * te, tm=tm)
    y_tiles = _experts(buf, tile_expert, n_used, wg, wu, wd, tm=te, tf=tf)
    return _combine(x, route, y_tiles, p1, p2, g, b, alpha=alpha, tm=tm)


def _pad_cols(w, width):
    return jnp.pad(w, ((0, 0), (0, width - w.shape[1])))


def _prep_in_weights(w_in, w_uq, w_ukv):
    sizes = (MLA_Q_RANK, MLA_KV_RANK, MLA_ROPE, W_RET, W_RET, W_RET, W_RET, W_MOBA, W_MOBA, W_MOBA)
    offs = np.cumsum((0,) + sizes)
    c_q, c_kv, k_rope, rq, rk, rv, rg, mq, mk, mv = (w_in[:, offs[n]:offs[n + 1]] for n in range(len(sizes)))
    w1 = jnp.concatenate([c_q, c_kv, _pad_cols(k_rope, LANES)], 1)
    w2 = jnp.concatenate([_pad_cols(w, RET_PAD) for w in (rq, rk, rv, rg)], 1)
    def head_slots(w):
        w = w.reshape(w.shape[0], MOBA_HEADS, MOBA_DH)
        return jnp.pad(w, ((0, 0), (0, 0), (0, MOBA_SLOT - MOBA_DH))).reshape(w.shape[0], MOBA_SLOTS)

    w3 = jnp.concatenate([head_slots(mq), head_slots(mk), _pad_cols(mv, MOBA_PAD)], 1)

    dq = MLA_NOPE + MLA_ROPE
    uq = w_uq.reshape(MLA_Q_RANK, MLA_HEADS, dq)
    zpad = jnp.zeros((MLA_Q_RANK, MLA_HEADS, MLA_SLOT - dq), w_uq.dtype)
    wuq = jnp.concatenate([uq, zpad], 2).reshape(MLA_Q_RANK, -1)
    ukv = w_ukv.reshape(MLA_KV_RANK, MLA_HEADS, MLA_NOPE + MLA_V)
    k_part = jnp.concatenate([ukv[:, :, :MLA_NOPE], jnp.zeros((MLA_KV_RANK, MLA_HEADS, MLA_SLOT - MLA_NOPE), w_ukv.dtype)], 2)
    wukv = jnp.concatenate([k_part.reshape(MLA_KV_RANK, -1), ukv[:, :, MLA_NOPE:].reshape(MLA_KV_RANK, -1)], 1)
    return tuple(w.astype(BF16) for w in (w1, w2, w3)), tuple(w.astype(BF16) for w in (wuq, wukv))


def _rope_key_placement():
    p = np.zeros((LANES, MLA_HEADS * MLA_SLOT), np.float32)
    for h in range(MLA_HEADS):
        for j in range(MLA_ROPE):
            p[j, h * MLA_SLOT + MLA_NOPE + j] = 1.0
    return jnp.asarray(p, BF16)


def _rope_tables(seq):
    def cos_sin(dim):
        pos = jnp.arange(seq, dtype=F32)
        inv = ROPE_THETA ** (-jnp.arange(0, dim, 2, dtype=F32) / dim)
        ang = pos[:, None] * inv[None, :]
        c, s = jnp.cos(ang), jnp.sin(ang)
        return jnp.concatenate([c, c], 1), jnp.concatenate([-s, s], 1)

    ca, sa = cos_sin(MLA_ROPE)
    cb, sb = cos_sin(RET_DK)
    scale = (MLA_NOPE + MLA_ROPE) ** -0.5 * LOG2E
    ones = jnp.ones((seq, MLA_NOPE), F32)
    zeros_n = jnp.zeros((seq, MLA_NOPE), F32)
    zpad = jnp.zeros((seq, MLA_SLOT - MLA_NOPE - MLA_ROPE), F32)
    cq = jnp.tile(jnp.concatenate([ones, ca, zpad], 1), (1, MLA_HEADS)) * scale
    sq = jnp.tile(jnp.concatenate([zeros_n, sa, zpad], 1), (1, MLA_HEADS)) * scale
    ck = _pad_cols(ca, LANES)
    sk = _pad_cols(sa, LANES)
    cr = _pad_cols(jnp.tile(cb, (1, RET_HEADS)), RET_PAD)
    sr = _pad_cols(jnp.tile(sb, (1, RET_HEADS)), RET_PAD)
    return cq, sq, ck, sk, cr, sr


def _ff_tile(width, preferred):
    return preferred if width % preferred == 0 else width


def kernel(x, ln_emb_g, ln_emb_b, w_in, q_norm_g, kv_norm_g, w_uq, w_ukv, beta_mla, beta_moba, w_o, ln1_g, ln1_b,
           ffn_w_gate, ffn_w_up, ffn_w_down, router, exp_w_gate, exp_w_up, exp_w_down, ln2_g, ln2_b):
    B, S, D = x.shape
    depth = w_in.shape[0]
    alpha = float((2 * depth) ** 0.25)
    assert S % MOBA_BLOCK == 0 and S % RET_CHUNK == 0
    tm = 512 if S % 512 == 0 else MOBA_BLOCK
    row = lambda v: v.reshape(1, -1).astype(F32)

    tables = _rope_tables(S)
    pk = _rope_key_placement()
    alibi_lanes = _moba_alibi_lanes(tm)
    xs = x.reshape(B * S, D)
    for l in range(depth):
        w123, wmla = _prep_in_weights(w_in[l], w_uq[l], w_ukv[l])
        weights = (*w123, row(q_norm_g[l]), row(kv_norm_g[l]), *wmla, pk, *alibi_lanes)
        ln = (row(ln_emb_g), row(ln_emb_b)) if l == 0 else None
        outs = _in_proj(xs, ln, weights, tables, seq=S, tm=tm)
        if l == 0:
            xs, *outs = outs
        qm, km, vm, rq, rk, rv, rg, mq, mk, mv, kmean = outs
        o_a = _mla_attention(qm, km, vm, batch=B, seq=S, tq=MOBA_BLOCK)
        o_b = _retention(rq, rk, rv, rg, batch=B, seq=S)
        o_c = _moba_attention(mq, mk, mv, kmean, batch=B, seq=S)
        wo = w_o[l].astype(BF16)
        routed = l % 2 == 1
        xs, *x_tiles = _out_proj(o_a, o_b, o_c, xs, row(beta_mla[l]), row(beta_moba[l]), wo,
                                 row(ln1_g[l]), row(ln1_b[l]), alpha=alpha, tm=tm, with_row_tiles=routed)
        j = l // 2
        if routed:
            xs = _moe(xs, *x_tiles, router[j], exp_w_gate[j].astype(BF16), exp_w_up[j].astype(BF16),
                      exp_w_down[j].astype(BF16), row(ln2_g[l]), row(ln2_b[l]), alpha=alpha, tm=tm,
                      te=2 * tm, tf=_ff_tile(exp_w_gate.shape[-1], 896))
        else:
            xs = _ffn(xs, ffn_w_gate[j].astype(BF16), ffn_w_up[j].astype(BF16), ffn_w_down[j].astype(BF16),
                      row(ln2_g[l]), row(ln2_b[l]), alpha=alpha, tm=tm, tf=_ff_tile(ffn_w_gate.shape[-1], 1408))
    return xs.reshape(B, S, D)
```
